```python
import math
import jax, jax.numpy as jnp
from jax import lax
import numpy as np

D_MODEL = 2048
BATCH = 1
SEQ = 8192
DEPTH = 4

GRID_W = 64
CTX_LEN = 256
N_MIXERS = 2
N_SSD_LAYERS = (DEPTH + 1) // 2
N_ATTN_LAYERS = DEPTH // 2

SSD_EXPAND = 2
SSD_D_INNER = SSD_EXPAND * D_MODEL
SSD_HEAD_DIM = 64
SSD_HEADS = SSD_D_INNER // SSD_HEAD_DIM
SSD_GROUPS = 8
SSD_STATE = 128
SSD_CHUNK = 128
SSD_CONV = 3
SSD_XBC = SSD_D_INNER + 2 * SSD_GROUPS * SSD_STATE
SSD_IN = SSD_D_INNER + SSD_XBC + 2 * SSD_HEADS

ATTN_HEAD_DIM = 128
ATTN_HEADS = D_MODEL // ATTN_HEAD_DIM
ATTN_KV_HEADS = 4
ATTN_GROUP = ATTN_HEADS // ATTN_KV_HEADS
ATTN_Q_DIM = ATTN_HEADS * ATTN_HEAD_DIM
ATTN_KV_DIM = ATTN_KV_HEADS * ATTN_HEAD_DIM
ATTN_QKV = ATTN_Q_DIM + 2 * ATTN_KV_DIM
ATTN_WINDOW = 128
ATTN_BLOCK = 128
ROPE_THETA = 10000.0
ROPE_AXIS_DIM = ATTN_HEAD_DIM // 2

D_FF = 5632
FFN_CONV = 3

NORM_EPS = 1e-6

kernel_name = "hybrid_ssd_swa_convffn_prefix_dit"


def _rms_norm(x, w):
    xf = x.astype(jnp.float32)
    y = xf * lax.rsqrt(jnp.mean(xf * xf, axis=-1, keepdims=True) + NORM_EPS)
    return y.astype(x.dtype) * w


def _modulation(cvec, w, b):
    m = jax.nn.silu(cvec) @ w + b
    return jnp.split(m, 6, axis=-1)


def _modulate(h, shift, scale):
    return h * (1.0 + scale) + shift


def _dwconv(x, w, b):
    k = w.shape[0]
    pad = k // 2
    y = lax.conv_general_dilated(x, w[:, None, :], window_strides=(1,), padding=[(pad, pad)],
                                 dimension_numbers=('NWC', 'WIO', 'NWC'),
                                 feature_group_count=x.shape[-1])
    return y + b


def _ssd_chunked(x, dA, B, C, h0):
    b, l, _, p = x.shape
    g, n, e = SSD_GROUPS, SSD_STATE, SSD_HEADS // SSD_GROUPS
    nc = l // SSD_CHUNK
    xc = x.reshape(b, nc, SSD_CHUNK, g, e, p)
    Bc = B.reshape(b, nc, SSD_CHUNK, g, n)
    Cc = C.reshape(b, nc, SSD_CHUNK, g, n)
    a_cs = jnp.cumsum(dA.reshape(b, nc, SSD_CHUNK, g, e), axis=2)
    lower = jnp.tril(jnp.ones((SSD_CHUNK, SSD_CHUNK), dtype=bool))[None, None, :, :, None, None]
    diff = a_cs[:, :, :, None] - a_cs[:, :, None, :]
    seg = jnp.exp(jnp.where(lower, diff, -jnp.inf))
    cb = jnp.einsum('bclgn,bcsgn->bclsg', Cc, Bc)
    w = (cb[..., None] * seg).astype(x.dtype)
    y_diag = jnp.einsum('bclsge,bcsgep->bclgep', w, xc)
    decay_to_end = jnp.exp(a_cs[:, :, -1:] - a_cs).astype(x.dtype)
    states = jnp.einsum('bclgn,bclgep->bcgepn', Bc, xc * decay_to_end[..., None])
    chunk_decay = jnp.exp(a_cs[:, :, -1]).astype(states.dtype)
    if h0 is None:
        h0 = jnp.zeros((b, g, e, p, n), states.dtype)

    def step(h, inp):
        dec, st = inp
        return dec[..., None, None] * h + st, h

    h_last, h_in = lax.scan(step, h0.astype(states.dtype),
                            (jnp.moveaxis(chunk_decay, 1, 0), jnp.moveaxis(states, 1, 0)))
    h_in = jnp.moveaxis(h_in, 0, 1)
    decay_from_start = jnp.exp(a_cs).astype(x.dtype)
    y_off = jnp.einsum('bclgn,bcgepn->bclgep', Cc, h_in) * decay_from_start[..., None]
    y = (y_diag + y_off).reshape(b, l, g * e, p)
    return y, h_last


def _ssd_project(h, w_in, conv_w, conv_b):
    b, l, _ = h.shape
    zxbcdt = h @ w_in
    o1 = SSD_D_INNER
    o2 = o1 + SSD_XBC
    o3 = o2 + SSD_HEADS
    z = zxbcdt[..., :o1]
    xbc = jax.nn.silu(_dwconv(zxbcdt[..., o1:o2], conv_w, conv_b))
    dt_f = zxbcdt[..., o2:o3]
    dt_b = zxbcdt[..., o3:]
    gn = SSD_GROUPS * SSD_STATE
    xs = xbc[..., :SSD_D_INNER].reshape(b, l, SSD_HEADS, SSD_HEAD_DIM)
    B = xbc[..., SSD_D_INNER:SSD_D_INNER + gn].reshape(b, l, SSD_GROUPS, SSD_STATE)
    C = xbc[..., SSD_D_INNER + gn:].reshape(b, l, SSD_GROUPS, SSD_STATE)
    return z, xs, B, C, dt_f, dt_b


def _ssd_direction(xs, B, C, dt_raw, dt_bias, a_log, h0, reverse):
    dt = jax.nn.softplus((dt_raw + dt_bias).astype(jnp.float32))
    dA = -jnp.exp(a_log.astype(jnp.float32)) * dt
    xdt = xs * dt[..., None].astype(xs.dtype)
    if reverse:
        xdt, dA, B, C = (jnp.flip(t, axis=1) for t in (xdt, dA, B, C))
    y, h_last = _ssd_chunked(xdt, dA, B, C, h0)
    if reverse:
        y = jnp.flip(y, axis=1)
    return y, h_last


def _ssd_mixer(h_lat, h_ctx, w_in, conv_w, conv_b, dt_bias_f, dt_bias_b, a_log_f, a_log_b,
               d_skip, norm_w, w_out, need_ctx_out):
    z_c, xs_c, B_c, C_c, dtf_c, dtb_c = _ssd_project(h_ctx, w_in, conv_w, conv_b)
    yf_c, hf_c = _ssd_direction(xs_c, B_c, C_c, dtf_c, dt_bias_f, a_log_f, None, False)
    yb_c, hb_c = _ssd_direction(xs_c, B_c, C_c, dtb_c, dt_bias_b, a_log_b, None, True)
    z, xs, B, C, dtf, dtb = _ssd_project(h_lat, w_in, conv_w, conv_b)
    yf, _ = _ssd_direction(xs, B, C, dtf, dt_bias_f, a_log_f, hf_c, False)
    yb, _ = _ssd_direction(xs, B, C, dtb, dt_bias_b, a_log_b, hb_c, True)

    def finish(y_f, y_b, x_s, zz):
        y = (y_f + y_b + x_s * d_skip[:, None]).reshape(zz.shape)
        return _rms_norm(y * jax.nn.silu(zz), norm_w) @ w_out

    out_lat = finish(yf, yb, xs, z)
    out_ctx = finish(yf_c, yb_c, xs_c, z_c) if need_ctx_out else None
    return out_lat, out_ctx


def _axial_angles(rows):
    row = jnp.repeat(jnp.arange(rows, dtype=jnp.float32), GRID_W)
    col = jnp.tile(jnp.arange(GRID_W, dtype=jnp.float32), rows)
    inv_freq = ROPE_THETA ** (-jnp.arange(0, ROPE_AXIS_DIM, 2, dtype=jnp.float32) / ROPE_AXIS_DIM)
    return row[:, None] * inv_freq[None, :], col[:, None] * inv_freq[None, :]


def _rope_half(x, ang):
    m = ang.shape[-1]
    shape = (ang.shape[0],) + (1,) * (x.ndim - 3) + (m,)
    cos = jnp.cos(ang).reshape(shape).astype(x.dtype)
    sin = jnp.sin(ang).reshape(shape).astype(x.dtype)
    x1, x2 = x[..., :m], x[..., m:]
    return jnp.concatenate([x1 * cos - x2 * sin, x2 * cos + x1 * sin], axis=-1)


def _rope_2d(x, ang_row, ang_col):
    return jnp.concatenate([_rope_half(x[..., :ROPE_AXIS_DIM], ang_row),
                            _rope_half(x[..., ROPE_AXIS_DIM:], ang_col)], axis=-1)


def _attn_qkv(h, w_qkv, q_gain, k_gain):
    b, l, _ = h.shape
    qkv = h @ w_qkv
    q = qkv[..., :ATTN_Q_DIM].reshape(b, l, ATTN_KV_HEADS, ATTN_GROUP, ATTN_HEAD_DIM)
    k = qkv[..., ATTN_Q_DIM:ATTN_Q_DIM + ATTN_KV_DIM].reshape(b, l, ATTN_KV_HEADS, ATTN_HEAD_DIM)
    v = qkv[..., ATTN_Q_DIM + ATTN_KV_DIM:].reshape(b, l, ATTN_KV_HEADS, ATTN_HEAD_DIM)
    return _rms_norm(q, q_gain), _rms_norm(k, k_gain), v


def _attn_mixer(h_lat, h_ctx, w_qkv, q_gain, k_gain, sinks, w_o, ang_row, ang_col, need_ctx_out):
    scale = ATTN_HEAD_DIM ** -0.5
    q_c, k_c, v_c = _attn_qkv(h_ctx, w_qkv, q_gain, k_gain)
    q, k, v = _attn_qkv(h_lat, w_qkv, q_gain, k_gain)
    q = _rope_2d(q, ang_row, ang_col)
    k = _rope_2d(k, ang_row, ang_col)
    b, n = q.shape[:2]
    lc = k_c.shape[1]
    nb = n // ATTN_BLOCK
    sink = sinks.astype(jnp.float32).reshape(ATTN_KV_HEADS, ATTN_GROUP)

    qb = q.reshape(b, nb, ATTN_BLOCK, ATTN_KV_HEADS, ATTN_GROUP, ATTN_HEAD_DIM)

    def band(t):
        tb = t.reshape(b, nb, ATTN_BLOCK, ATTN_KV_HEADS, ATTN_HEAD_DIM)
        tp = jnp.pad(tb, ((0, 0), (1, 1), (0, 0), (0, 0), (0, 0)))
        return jnp.concatenate([tp[:, :-2], tp[:, 1:-1], tp[:, 2:]], axis=2)

    k_band, v_band = band(k), band(v)
    qi = jnp.arange(ATTN_BLOCK)[:, None]
    kj = jnp.arange(3 * ATTN_BLOCK)[None, :]
    kpos = (jnp.arange(nb)[:, None, None] - 1) * ATTN_BLOCK + kj[None]
    band_mask = (jnp.abs(kj - ATTN_BLOCK - qi) <= ATTN_WINDOW)[None] & (kpos >= 0) & (kpos < n)

    s_ctx = jnp.einsum('bkqhgd,bshd->bkhgqs', qb, k_c).astype(jnp.float32) * scale
    s_band = jnp.einsum('bkqhgd,bkshd->bkhgqs', qb, k_band).astype(jnp.float32) * scale
    s_band = jnp.where(band_mask[None, :, None, None], s_band, -jnp.inf)
    s_sink = jnp.broadcast_to(sink[None, None, :, :, None, None], s_band.shape[:-1] + (1,))
    p = jax.nn.softmax(jnp.concatenate([s_ctx, s_band, s_sink], axis=-1), axis=-1)
    p_ctx = p[..., :lc].astype(v.dtype)
    p_band = p[..., lc:-1].astype(v.dtype)
    o = (jnp.einsum('bkhgqs,bshd->bkqhgd', p_ctx, v_c)
         + jnp.einsum('bkhgqs,bkshd->bkqhgd', p_band, v_band))
    out_lat = o.reshape(b, n, ATTN_Q_DIM) @ w_o

    if need_ctx_out:
        s = jnp.einsum('bqhgd,bshd->bhgqs', q_c, k_c).astype(jnp.float32) * scale
        s_sk = jnp.broadcast_to(sink[None, :, :, None, None], s.shape[:-1] + (1,))
        pc = jax.nn.softmax(jnp.concatenate([s, s_sk], axis=-1), axis=-1)[..., :-1].astype(v_c.dtype)
        oc = jnp.einsum('bhgqs,bshd->bqhgd', pc, v_c)
        out_ctx = oc.reshape(b, lc, ATTN_Q_DIM) @ w_o
    else:
        out_ctx = None
    return out_lat, out_ctx


def _conv_ffn(h, w_up, conv_w, conv_b, w_down):
    u = _dwconv(h @ w_up, conv_w, conv_b)
    val, gate = u[..., :D_FF], u[..., D_FF:]
    return (jax.nn.silu(gate) * val) @ w_down


def setup_inputs(seed: int = 0) -> dict:
    key = jax.random.key(seed)
    ks = jax.random.split(key, 32)
    f32 = jnp.float32

    def nrm(k, shape, s):
        return jax.random.normal(k, shape, f32) * s

    L, S, A = DEPTH, N_SSD_LAYERS, N_ATTN_LAYERS

    def dt_bias(k):
        dt = jnp.exp(jax.random.uniform(k, (S, SSD_HEADS), f32, math.log(1e-3), math.log(1e-1)))
        return dt + jnp.log(-jnp.expm1(-dt))

    return {
        "x": nrm(ks[0], (BATCH, SEQ, D_MODEL), 1.0),
        "c": nrm(ks[1], (BATCH, D_MODEL), 1.0),
        "ctx": nrm(ks[2], (BATCH, CTX_LEN, D_MODEL), 1.0),
        "c_ctx": nrm(ks[3], (D_MODEL,), 1.0),
        "ada_w": nrm(ks[4], (L, D_MODEL, 6 * D_MODEL), 0.5 * D_MODEL ** -0.5),
        "ada_b": nrm(ks[5], (L, 6 * D_MODEL), 0.01),
        "norm1_w": 1.0 + nrm(ks[6], (L, D_MODEL), 0.05),
        "norm2_w": 1.0 + nrm(ks[7], (L, D_MODEL), 0.05),
        "ssd_w_in": nrm(ks[8], (S, D_MODEL, SSD_IN), D_MODEL ** -0.5),
        "ssd_conv_w": nrm(ks[9], (S, SSD_CONV, SSD_XBC), SSD_CONV ** -0.5),
        "ssd_conv_b": nrm(ks[10], (S, SSD_XBC), 0.01),
        "ssd_dt_bias_f": dt_bias(ks[11]),
        "ssd_dt_bias_b": dt_bias(ks[12]),
        "ssd_a_log_f": jnp.log(jax.random.uniform(ks[13], (S, SSD_HEADS), f32, 1.0, 16.0)),
        "ssd_a_log_b": jnp.log(jax.random.uniform(ks[14], (S, SSD_HEADS), f32, 1.0, 16.0)),
        "ssd_d": 1.0 + nrm(ks[15], (S, SSD_HEADS), 0.1),
        "ssd_norm_w": 1.0 + nrm(ks[16], (S, SSD_D_INNER), 0.05),
        "ssd_w_out": nrm(ks[17], (S, SSD_D_INNER, D_MODEL), SSD_D_INNER ** -0.5),
        "attn_w_qkv": nrm(ks[18], (A, D_MODEL, ATTN_QKV), D_MODEL ** -0.5),
        "attn_q_gain": 1.0 + nrm(ks[19], (A, ATTN_HEAD_DIM), 0.05),
        "attn_k_gain": 1.0 + nrm(ks[20], (A, ATTN_HEAD_DIM), 0.05),
        "attn_sinks": nrm(ks[21], (A, ATTN_HEADS), 0.5),
        "attn_w_o": nrm(ks[22], (A, ATTN_Q_DIM, D_MODEL), ATTN_Q_DIM ** -0.5),
        "ffn_w_up": nrm(ks[23], (L, D_MODEL, 2 * D_FF), D_MODEL ** -0.5),
        "ffn_conv_w": nrm(ks[24], (L, FFN_CONV, 2 * D_FF), FFN_CONV ** -0.5),
        "ffn_conv_b": nrm(ks[25], (L, 2 * D_FF), 0.01),
        "ffn_w_down": nrm(ks[26], (L, D_FF, D_MODEL), D_FF ** -0.5),
    }


def reference(x, c, ctx, c_ctx, ada_w, ada_b, norm1_w, norm2_w, ssd_w_in, ssd_conv_w, ssd_conv_b,
              ssd_dt_bias_f, ssd_dt_bias_b, ssd_a_log_f, ssd_a_log_b, ssd_d, ssd_norm_w, ssd_w_out,
              attn_w_qkv, attn_q_gain, attn_k_gain, attn_sinks, attn_w_o,
              ffn_w_up, ffn_conv_w, ffn_conv_b, ffn_w_down):
    n = x.shape[1]
    rows = n // GRID_W
    ang_row, ang_col = _axial_angles(rows)
    xc = ctx
    for i in range(DEPTH):
        last = i == DEPTH - 1
        sh1, sc1, g1, sh2, sc2, g2 = [t[:, None, :] for t in _modulation(c, ada_w[i], ada_b[i])]
        csh1, csc1, cg1, csh2, csc2, cg2 = _modulation(c_ctx, ada_w[i], ada_b[i])
        h = _modulate(_rms_norm(x, norm1_w[i]), sh1, sc1)
        hc = _modulate(_rms_norm(xc, norm1_w[i]), csh1, csc1)
        j = i // N_MIXERS
        if i % N_MIXERS == 0:
            mix, mix_c = _ssd_mixer(h, hc, ssd_w_in[j], ssd_conv_w[j], ssd_conv_b[j],
                                    ssd_dt_bias_f[j], ssd_dt_bias_b[j], ssd_a_log_f[j], ssd_a_log_b[j],
                                    ssd_d[j], ssd_norm_w[j], ssd_w_out[j], not last)
        else:
            mix, mix_c = _attn_mixer(h, hc, attn_w_qkv[j], attn_q_gain[j], attn_k_gain[j],
                                     attn_sinks[j], attn_w_o[j], ang_row, ang_col, not last)
        x = x + g1 * mix
        h = _modulate(_rms_norm(x, norm2_w[i]), sh2, sc2)
        x = x + g2 * _conv_ffn(h, ffn_w_up[i], ffn_conv_w[i], ffn_conv_b[i], ffn_w_down[i])
        if not last:
            xc = xc + cg1 * mix_c
            hc = _modulate(_rms_norm(xc, norm2_w[i]), csh2, csc2)
            xc = xc + cg2 * _conv_ffn(hc, ffn_w_up[i], ffn_conv_w[i], ffn_conv_b[i], ffn_w_down[i])
    return x
```

```python
import functools
import math

import jax
import jax.numpy as jnp
from jax import lax
from jax.experimental import pallas as pl
from jax.experimental.pallas import tpu as pltpu

F32 = jnp.float32
BF16 = jnp.bfloat16

NORM_EPS = 1e-6
GRID_W = 64
ROPE_THETA = 10000.0

SSD_HEAD_DIM = 64
SSD_GROUPS = 8
SSD_STATE = 128
SSD_CHUNK = 128
ATTN_HEAD_DIM = 128
ATTN_KV_HEADS = 4
ATTN_BLOCK = 128

ROW_TILE = 768
HALO = 16
VMEM_LIMIT = 56 * 1024 * 1024


def _cparams(sem):
    return pltpu.CompilerParams(dimension_semantics=sem, vmem_limit_bytes=VMEM_LIMIT)


def _silu(v):
    return v / (1.0 + jnp.exp(-v))


def _softplus(v):
    return jnp.maximum(v, 0.0) + jnp.log(1.0 + jnp.exp(-jnp.abs(v)))


def _modulation_kernel(c_ref, w_ref, b_ref, o_ref, *, kc):
    d = c_ref.shape[0]
    tn = w_ref.shape[2]

    def body(k, acc):
        a0, a1 = acc
        k0 = pl.multiple_of(k * kc, kc)
        cv = c_ref[pl.ds(k0, kc), :]
        sv = _silu(cv)
        w = w_ref[0, pl.ds(k0, kc), :]
        a0 = a0 + jnp.sum(w * sv[:, 0:1], axis=0, keepdims=True)
        a1 = a1 + jnp.sum(w * sv[:, 1:2], axis=0, keepdims=True)
        return a0, a1

    z = jnp.zeros((1, tn), F32)
    a0, a1 = lax.fori_loop(0, d // kc, body, (z, z))
    bias = b_ref[0]
    o_ref[0, 0:1, :] = a0 + bias
    o_ref[0, 1:2, :] = a1 + bias


def _modulation(c2, ada_w, ada_b):
    depth, d, n = ada_w.shape
    tn = 1024
    return pl.pallas_call(
        functools.partial(_modulation_kernel, kc=256),
        out_shape=jax.ShapeDtypeStruct((depth, 2, n), F32),
        grid=(depth, n // tn),
        in_specs=[
            pl.BlockSpec((d, 2), lambda l, j: (0, 0)),
            pl.BlockSpec((1, d, tn), lambda l, j: (l, 0, j)),
            pl.BlockSpec((1, 1, tn), lambda l, j: (l, 0, j)),
        ],
        out_specs=pl.BlockSpec((1, 2, tn), lambda l, j: (l, 0, j)),
        compiler_params=_cparams(("parallel", "parallel")),
        name="adaln_modulation",
    )(c2, ada_w, ada_b.reshape(depth, 1, n))


def _norm_mod_rows(x_ref, tab_ref, a_ref, i, *, n_lat, sub):
    tm = x_ref.shape[0]
    w = tab_ref[0:1, :]
    mult_lat = w * (1.0 + tab_ref[1:2, :])
    mult_ctx = w * (1.0 + tab_ref[3:4, :])
    shift_lat = tab_ref[2:3, :]
    shift_ctx = tab_ref[4:5, :]

    def body(r, carry):
        r0 = pl.multiple_of(r * sub, sub)
        is_ctx = (i * tm + r0) >= n_lat
        mult = jnp.where(is_ctx, mult_ctx, mult_lat)
        shift = jnp.where(is_ctx, shift_ctx, shift_lat)
        x = x_ref[pl.ds(r0, sub), :]
        ms = jnp.mean(x * x, axis=-1, keepdims=True)
        xh = x * lax.rsqrt(ms + NORM_EPS)
        a_ref[pl.ds(r0, sub), :] = (xh * mult + shift).astype(BF16)
        return carry

    lax.fori_loop(0, tm // sub, body, 0)


def _nmm_plain_kernel(x_ref, tab_ref, w_ref, o_ref, a_ref, *, n_lat):
    i, j = pl.program_id(0), pl.program_id(1)

    @pl.when(j == 0)
    def _():
        _norm_mod_rows(x_ref, tab_ref, a_ref, i, n_lat=n_lat, sub=128)

    o_ref[...] = jnp.dot(a_ref[...], w_ref[...], preferred_element_type=F32).astype(o_ref.dtype)


def _nmm_ssd_kernel(x_ref, tab_ref, w_ref, wdt_ref, o_ref, dt_ref, a_ref, *, n_lat):
    i, j = pl.program_id(0), pl.program_id(1)

    @pl.when(j == 0)
    def _():
        _norm_mod_rows(x_ref, tab_ref, a_ref, i, n_lat=n_lat, sub=128)
        dt_ref[...] = jnp.dot(a_ref[...], wdt_ref[...], preferred_element_type=F32)

    o_ref[...] = jnp.dot(a_ref[...], w_ref[...], preferred_element_type=F32).astype(o_ref.dtype)


def _rope_swap(t):
    lane = lax.broadcasted_iota(jnp.int32, t.shape, 1)
    first = (lane % 64) < 32
    return jnp.where(first, pltpu.roll(t, 96, 1), pltpu.roll(t, 32, 1))


def _nmm_qkv_kernel(x_ref, tab_ref, w_ref, gain_ref, cos_ref, sin_ref, o_ref, a_ref, *,
                    n_lat, n_q_tiles, q_scale):
    i, j = pl.program_id(0), pl.program_id(1)
    hd = ATTN_HEAD_DIM

    @pl.when(j == 0)
    def _():
        _norm_mod_rows(x_ref, tab_ref, a_ref, i, n_lat=n_lat, sub=128)

    acc = jnp.dot(a_ref[...], w_ref[...], preferred_element_type=F32)
    tn = acc.shape[1]

    @pl.when(j <= n_q_tiles)
    def _():
        is_q = j < n_q_tiles
        gain = jnp.where(is_q, gain_ref[0:1, :], gain_ref[1:2, :])
        post = jnp.where(is_q, q_scale, 1.0)
        cos = cos_ref[...]
        sin = sin_ref[...]
        for h in range(tn // hd):
            t = acc[:, h * hd:(h + 1) * hd]
            ms = jnp.mean(t * t, axis=-1, keepdims=True)
            t = t * lax.rsqrt(ms + NORM_EPS) * gain
            t = (t * cos + _rope_swap(t) * sin) * post
            o_ref[:, h * hd:(h + 1) * hd] = t.astype(o_ref.dtype)

    @pl.when(j > n_q_tiles)
    def _():
        o_ref[...] = acc.astype(o_ref.dtype)


def _norm_mod_matmul(x, tab, w, *, n_lat, tn, kind, extra=()):
    m, d = x.shape
    n = w.shape[1]
    tm = ROW_TILE
    grid = (m // tm, n // tn)
    x_spec = pl.BlockSpec((tm, d), lambda i, j: (i, 0))
    tab_spec = pl.BlockSpec((8, d), lambda i, j: (0, 0))
    w_spec = pl.BlockSpec((d, tn), lambda i, j: (0, j))
    o_spec = pl.BlockSpec((tm, tn), lambda i, j: (i, j))
    scratch = [pltpu.VMEM((tm, d), BF16)]
    cp = _cparams(("parallel", "arbitrary"))
    if kind == "plain":
        return pl.pallas_call(
            functools.partial(_nmm_plain_kernel, n_lat=n_lat),
            out_shape=jax.ShapeDtypeStruct((m, n), BF16),
            grid=grid, in_specs=[x_spec, tab_spec, w_spec], out_specs=o_spec,
            scratch_shapes=scratch, compiler_params=cp, name="norm_mod_matmul",
        )(x, tab, w)
    if kind == "ssd":
        (wdt,) = extra
        ndt = wdt.shape[1]
        return pl.pallas_call(
            functools.partial(_nmm_ssd_kernel, n_lat=n_lat),
            out_shape=(jax.ShapeDtypeStruct((m, n), BF16), jax.ShapeDtypeStruct((m, ndt), F32)),
            grid=grid,
            in_specs=[x_spec, tab_spec, w_spec, pl.BlockSpec((d, ndt), lambda i, j: (0, 0))],
            out_specs=(o_spec, pl.BlockSpec((tm, ndt), lambda i, j: (i, 0))),
            scratch_shapes=scratch, compiler_params=cp, name="ssd_in_proj",
        )(x, tab, w, wdt)
    gains, cos, sin, n_q_tiles, q_scale = extra
    return pl.pallas_call(
        functools.partial(_nmm_qkv_kernel, n_lat=n_lat, n_q_tiles=n_q_tiles, q_scale=q_scale),
        out_shape=jax.ShapeDtypeStruct((m, n), BF16),
        grid=grid,
        in_specs=[x_spec, tab_spec, w_spec,
                  pl.BlockSpec((8, ATTN_HEAD_DIM), lambda i, j: (0, 0)),
                  pl.BlockSpec((tm, ATTN_HEAD_DIM), lambda i, j: (i, 0)),
                  pl.BlockSpec((tm, ATTN_HEAD_DIM), lambda i, j: (i, 0))],
        out_specs=o_spec,
        scratch_shapes=scratch, compiler_params=cp, name="attn_qkv_proj",
    )(x, tab, w, gains, cos, sin)


def _row_gate(g_ref, i, tm, n_lat):
    rows = i * tm + lax.broadcasted_iota(jnp.int32, (tm, 1), 0)
    return jnp.where(rows >= n_lat, g_ref[1:2, :], g_ref[0:1, :])


def _mm_res_kernel(a_ref, w_ref, x_ref, g_ref, o_ref, *, n_lat):
    i = pl.program_id(0)
    tm = a_ref.shape[0]
    acc = jnp.dot(a_ref[...], w_ref[...], preferred_element_type=F32)
    o_ref[...] = x_ref[...] + _row_gate(g_ref, i, tm, n_lat) * acc


def _matmul_residual(a, w, x, gates, *, n_lat, tn):
    m, k = a.shape
    n = w.shape[1]
    tm = ROW_TILE
    return pl.pallas_call(
        functools.partial(_mm_res_kernel, n_lat=n_lat),
        out_shape=jax.ShapeDtypeStruct((m, n), F32),
        grid=(m // tm, n // tn),
        in_specs=[pl.BlockSpec((tm, k), lambda i, j: (i, 0)),
                  pl.BlockSpec((k, tn), lambda i, j: (0, j)),
                  pl.BlockSpec((tm, tn), lambda i, j: (i, j)),
                  pl.BlockSpec((8, tn), lambda i, j: (0, j))],
        out_specs=pl.BlockSpec((tm, tn), lambda i, j: (i, j)),
        compiler_params=_cparams(("parallel", "parallel")),
        name="matmul_gated_residual",
    )(a, w, x, gates)


def _conv3_rows(cur_ref, prev_ref, next_ref, cw_ref, cb_ref, prev_ok, next_ok):
    v = cur_ref[...].astype(F32)
    rows = v.shape[0]
    rid = lax.broadcasted_iota(jnp.int32, (rows, 1), 0)
    up = jnp.where(rid == 0, prev_ref[HALO - 1:HALO, :].astype(F32), pltpu.roll(v, 1, 0))
    dn = jnp.where(rid == rows - 1, next_ref[0:1, :].astype(F32), pltpu.roll(v, rows - 1, 0))
    up = jnp.where(prev_ok, up, 0.0)
    dn = jnp.where(next_ok, dn, 0.0)
    return cw_ref[0:1, :] * up + cw_ref[1:2, :] * v + cw_ref[2:3, :] * dn + cb_ref[...]


def _ffn_down_kernel(val_ref, gat_ref, vp_ref, vn_ref, gp_ref, gn_ref, cwv_ref, cwg_ref,
                     cbv_ref, cbg_ref, w_ref, x_ref, g_ref, o_ref, acc_ref, *, n_lat, m_total):
    i, k = pl.program_id(0), pl.program_id(1)
    tm = val_ref.shape[0]
    rows = i * tm + lax.broadcasted_iota(jnp.int32, (tm, 1), 0)
    prev_ok = (rows != 0) & (rows != n_lat)
    next_ok = (rows != n_lat - 1) & (rows != m_total - 1)
    val = _conv3_rows(val_ref, vp_ref, vn_ref, cwv_ref, cbv_ref, prev_ok, next_ok)
    gat = _conv3_rows(gat_ref, gp_ref, gn_ref, cwg_ref, cbg_ref, prev_ok, next_ok)
    a = (_silu(gat) * val).astype(BF16)
    part = jnp.dot(a, w_ref[...], preferred_element_type=F32)

    @pl.when(k == 0)
    def _():
        acc_ref[...] = part

    @pl.when(k > 0)
    def _():
        acc_ref[...] += part

    @pl.when(k == pl.num_programs(1) - 1)
    def _():
        o_ref[...] = x_ref[...] + _row_gate(g_ref, i, tm, n_lat) * acc_ref[...]


def _ffn_down(u, conv_w, conv_b, w_down, x, gates, *, n_lat, m_out):
    m, two_f = u.shape
    f = two_f // 2
    d = w_down.shape[1]
    tm, tk = ROW_TILE, 512
    nk = f // tk
    hb = tm // HALO
    last_hb = m // HALO - 1
    prev_map = lambda off: (lambda i, k: (jnp.maximum(i * hb - 1, 0), k + off))
    next_map = lambda off: (lambda i, k: (jnp.minimum((i + 1) * hb, last_hb), k + off))
    cb2 = conv_b.reshape(1, two_f)
    return pl.pallas_call(
        functools.partial(_ffn_down_kernel, n_lat=n_lat, m_total=m),
        out_shape=jax.ShapeDtypeStruct((m_out, d), F32),
        grid=(m // tm, nk),
        in_specs=[
            pl.BlockSpec((tm, tk), lambda i, k: (i, k)),
            pl.BlockSpec((tm, tk), lambda i, k: (i, k + nk)),
            pl.BlockSpec((HALO, tk), prev_map(0)),
            pl.BlockSpec((HALO, tk), next_map(0)),
            pl.BlockSpec((HALO, tk), prev_map(nk)),
            pl.BlockSpec((HALO, tk), next_map(nk)),
            pl.BlockSpec((3, tk), lambda i, k: (0, k)),
            pl.BlockSpec((3, tk), lambda i, k: (0, k + nk)),
            pl.BlockSpec((1, tk), lambda i, k: (0, k)),
            pl.BlockSpec((1, tk), lambda i, k: (0, k + nk)),
            pl.BlockSpec((tk, d), lambda i, k: (k, 0)),
            pl.BlockSpec((tm, d), lambda i, k: (i, 0)),
            pl.BlockSpec((8, d), lambda i, k: (0, 0)),
        ],
        out_specs=pl.BlockSpec((tm, d), lambda i, k: (i, 0)),
        scratch_shapes=[pltpu.VMEM((tm, d), F32)],
        compiler_params=_cparams(("parallel", "arbitrary")),
        name="convffn_down",
    )(u, u, u, u, u, u, conv_w, conv_w, cb2, cb2, w_down, x, gates)


def _ssd_scan_chunk(xs_ref, bc_ref, dt_raw, bias_ref, alog_ref, y_ref, state_ref, xf_ref, col_ref,
                    *, reverse, hoff, first):
    L = SSD_CHUNK
    P = SSD_HEAD_DIM
    G = SSD_GROUPS
    N = SSD_STATE
    H = xs_ref.shape[1] // P
    E = H // G
    EP = E * P

    @pl.when(first)
    def _():
        state_ref[...] = jnp.zeros_like(state_ref)

    dt = _softplus(dt_raw + bias_ref[...])
    da = -jnp.exp(alog_ref[...]) * dt
    li = lax.broadcasted_iota(jnp.int32, (L, L), 0)
    si = lax.broadcasted_iota(jnp.int32, (L, L), 1)
    keep = (si >= li) if reverse else (si <= li)
    tri = keep.astype(F32)
    acs = jnp.dot(tri, da, preferred_element_type=F32, precision=lax.Precision.HIGHEST)
    edge = 0 if reverse else L - 1
    acs_edge = acs[edge:edge + 1, :]
    e_in = jnp.exp(acs)
    f_out = dt * jnp.exp(acs_edge - acs)
    acs_t = acs.T
    dt_t = dt.T
    for g in range(G):
        lo = hoff + g * E
        col_ref[0, g, :, 0:E] = acs[:, lo:lo + E]
        col_ref[1, g, :, 0:E] = e_in[:, lo:lo + E]
        col_ref[2, g, :, 0:E] = f_out[:, lo:lo + E]
    cd_all = jnp.exp(acs_edge)

    for g in range(G):
        bg = bc_ref[:, g * N:(g + 1) * N]
        cg = bc_ref[:, (G + g) * N:(G + g + 1) * N]
        cb = lax.dot_general(cg, bg, (((1,), (1,)), ((), ())), preferred_element_type=F32)
        h_in = state_ref[g]
        y_off = jnp.dot(cg, h_in.astype(BF16), preferred_element_type=F32)
        cols_a = col_ref[0, g]
        cols_e = col_ref[1, g]
        cols_f = col_ref[2, g]
        for e in range(E):
            h = g * E + e
            lane = hoff + h
            col = cols_a[:, e:e + 1]
            row = acs_t[lane:lane + 1, :]
            seg = jnp.exp(jnp.where(keep, col - row, -jnp.inf))
            w = (cb * seg * dt_t[lane:lane + 1, :]).astype(BF16)
            xe = xs_ref[:, h * P:(h + 1) * P]
            yd = jnp.dot(w, xe, preferred_element_type=F32)
            y_ref[:, h * P:(h + 1) * P] = yd + y_off[:, e * P:(e + 1) * P] * cols_e[:, e:e + 1]
            xf_ref[:, e * P:(e + 1) * P] = (xe.astype(F32) * cols_f[:, e:e + 1]).astype(BF16)
        st = lax.dot_general(bg, xf_ref[...], (((0,), (0,)), ((), ())),
                             preferred_element_type=F32)
        for e in range(E):
            lane = hoff + g * E + e
            cd = cd_all[0:1, lane:lane + 1]
            state_ref[g, :, e * P:(e + 1) * P] = (cd * h_in[:, e * P:(e + 1) * P]
                                                  + st[:, e * P:(e + 1) * P])


def _ssd_fwd_kernel(xp_ref, xpp_ref, xpn_ref, bp_ref, bpp_ref, bpn_ref, dt_ref,
                    cwx_ref, cwb_ref, cbx_ref, cbb_ref, bias_ref, alog_ref,
                    xs_ref, bc_ref, y_ref, state_ref, xf_ref, col_ref, yacc_ref, *, n_lat_chunks,
                    n_chunks):
    t = pl.program_id(0)
    c = (t + n_lat_chunks) % n_chunks
    L = SSD_CHUNK
    CB = 512
    rid = lax.broadcasted_iota(jnp.int32, (L, 1), 0)
    prev_ok = (rid != 0) | ((c != 0) & (c != n_lat_chunks))
    next_ok = (rid != L - 1) | ((c != n_lat_chunks - 1) & (c != n_chunks - 1))

    def conv_part(src, prv, nxt, cw, cb, dst):
        width = src.shape[1]
        for c0 in range(0, width, CB):
            sl = slice(c0, c0 + CB)
            cv = _conv3_rows(src.at[:, sl], prv.at[:, sl], nxt.at[:, sl], cw.at[:, sl], cb.at[:, sl],
                             prev_ok, next_ok)
            dst[:, sl] = _silu(cv).astype(BF16)

    conv_part(xp_ref, xpp_ref, xpn_ref, cwx_ref, cbx_ref, xs_ref)
    conv_part(bp_ref, bpp_ref, bpn_ref, cwb_ref, cbb_ref, bc_ref)
    _ssd_scan_chunk(xs_ref, bc_ref, dt_ref[...], bias_ref, alog_ref, yacc_ref, state_ref, xf_ref,
                    col_ref, reverse=False, hoff=0, first=(t == 0))
    y_ref[...] = yacc_ref[...].astype(y_ref.dtype)


def _ssd_bwd_kernel(xs_ref, bc_ref, dt_ref, yf_ref, z_ref, dskip_ref, nw_ref, bias_ref, alog_ref,
                    g_ref, state_ref, xf_ref, col_ref, yacc_ref, *, hoff):
    t = pl.program_id(0)
    _ssd_scan_chunk(xs_ref, bc_ref, dt_ref[...], bias_ref, alog_ref, yacc_ref, state_ref, xf_ref,
                    col_ref, reverse=True, hoff=hoff, first=(t == 0))
    y = (yacc_ref[...] + yf_ref[...].astype(F32)
         + xs_ref[...].astype(F32) * dskip_ref[...])
    y = y * _silu(z_ref[...].astype(F32))
    ms = jnp.mean(y * y, axis=-1, keepdims=True)
    g_ref[...] = (y * lax.rsqrt(ms + NORM_EPS) * nw_ref[...]).astype(g_ref.dtype)


def _ssd_mixer(zx, dt, conv_w, conv_b, bias_fb, alog_fb, dskip, norm_w, *, n_lat):
    m = zx.shape[0]
    di = norm_w.shape[0]
    gn2 = 2 * SSD_GROUPS * SSD_STATE
    L = SSD_CHUNK
    n_chunks = m // L
    n_lat_chunks = n_lat // L
    hb = L // HALO
    last_hb = m // HALO - 1
    x_cb = 1
    bc_cb = (2 * di) // gn2
    cf = lambda t: (t + n_lat_chunks) % n_chunks
    prev_rows = lambda t: jnp.maximum(cf(t) * hb - 1, 0)
    next_rows = lambda t: jnp.minimum((cf(t) + 1) * hb, last_hb)
    cb2 = conv_b.reshape(1, -1)
    scratch = [pltpu.VMEM((SSD_GROUPS, SSD_STATE, di // SSD_GROUPS), F32),
               pltpu.VMEM((L, di // SSD_GROUPS), BF16),
               pltpu.VMEM((3, SSD_GROUPS, L, 128), F32),
               pltpu.VMEM((L, di), F32)]
    xs, bc, yf = pl.pallas_call(
        functools.partial(_ssd_fwd_kernel, n_lat_chunks=n_lat_chunks, n_chunks=n_chunks),
        out_shape=(jax.ShapeDtypeStruct((m, di), BF16), jax.ShapeDtypeStruct((m, gn2), BF16),
                   jax.ShapeDtypeStruct((m, di), BF16)),
        grid=(n_chunks,),
        in_specs=[
            pl.BlockSpec((L, di), lambda t: (cf(t), x_cb)),
            pl.BlockSpec((HALO, di), lambda t: (prev_rows(t), x_cb)),
            pl.BlockSpec((HALO, di), lambda t: (next_rows(t), x_cb)),
            pl.BlockSpec((L, gn2), lambda t: (cf(t), bc_cb)),
            pl.BlockSpec((HALO, gn2), lambda t: (prev_rows(t), bc_cb)),
            pl.BlockSpec((HALO, gn2), lambda t: (next_rows(t), bc_cb)),
            pl.BlockSpec((L, 128), lambda t: (cf(t), 0)),
            pl.BlockSpec((3, di), lambda t: (0, 0)),
            pl.BlockSpec((3, gn2), lambda t: (0, di // gn2)),
            pl.BlockSpec((1, di), lambda t: (0, 0)),
            pl.BlockSpec((1, gn2), lambda t: (0, di // gn2)),
            pl.BlockSpec((1, 128), lambda t: (0, 0)),
            pl.BlockSpec((1, 128), lambda t: (0, 0)),
        ],
        out_specs=(pl.BlockSpec((L, di), lambda t: (cf(t), 0)),
                   pl.BlockSpec((L, gn2), lambda t: (cf(t), 0)),
                   pl.BlockSpec((L, di), lambda t: (cf(t), 0))),
        scratch_shapes=scratch,
        compiler_params=_cparams(("arbitrary",)),
        name="ssd_conv_scan_fwd",
    )(zx, zx, zx, zx, zx, zx, dt, conv_w, conv_w, cb2, cb2, bias_fb, alog_fb)

    cbk = lambda t: n_chunks - 1 - t
    n_heads = di // SSD_HEAD_DIM
    return pl.pallas_call(
        functools.partial(_ssd_bwd_kernel, hoff=n_heads),
        out_shape=jax.ShapeDtypeStruct((m, di), BF16),
        grid=(n_chunks,),
        in_specs=[
            pl.BlockSpec((L, di), lambda t: (cbk(t), 0)),
            pl.BlockSpec((L, gn2), lambda t: (cbk(t), 0)),
            pl.BlockSpec((L, 128), lambda t: (cbk(t), 0)),
            pl.BlockSpec((L, di), lambda t: (cbk(t), 0)),
            pl.BlockSpec((L, di), lambda t: (cbk(t), 0)),
            pl.BlockSpec((1, di), lambda t: (0, 0)),
            pl.BlockSpec((1, di), lambda t: (0, 0)),
            pl.BlockSpec((1, 128), lambda t: (0, 0)),
            pl.BlockSpec((1, 128), lambda t: (0, 0)),
        ],
        out_specs=pl.BlockSpec((L, di), lambda t: (cbk(t), 0)),
        scratch_shapes=scratch,
        compiler_params=_cparams(("arbitrary",)),
        name="ssd_scan_bwd_finish",
    )(xs, bc, dt, yf, zx, dskip, norm_w.reshape(1, di), bias_fb, alog_fb)


def _attn_kernel(sink_ref, q_ref, kc_ref, vc_ref, kp_ref, ko_ref, kn_ref, vp_ref, vo_ref, vn_ref,
                 o_ref, *, n_lat_blocks):
    kvh, qb = pl.program_id(0), pl.program_id(1)
    blk = ATTN_BLOCK
    hd = ATTN_HEAD_DIM
    grp = q_ref.shape[1] // hd
    rows = grp * blk
    q = jnp.concatenate([q_ref[:, g * hd:(g + 1) * hd] for g in range(grp)], axis=0)
    nt = (((1,), (1,)), ((), ()))
    s_c = lax.dot_general(q, kc_ref[...], nt, preferred_element_type=F32)
    s_p = lax.dot_general(q, kp_ref[...], nt, preferred_element_type=F32)
    s_o = lax.dot_general(q, ko_ref[...], nt, preferred_element_type=F32)
    s_n = lax.dot_general(q, kn_ref[...], nt, preferred_element_type=F32)
    qi = lax.broadcasted_iota(jnp.int32, (rows, blk), 0) % blk
    kj = lax.broadcasted_iota(jnp.int32, (rows, blk), 1)
    is_lat = qb < n_lat_blocks
    ok_p = is_lat & (qb > 0)
    ok_n = is_lat & (qb < n_lat_blocks - 1)
    neg = -jnp.inf
    s_p = jnp.where(ok_p & (kj >= qi), s_p, neg)
    s_o = jnp.where(is_lat, s_o, neg)
    s_n = jnp.where(ok_n & (kj <= qi), s_n, neg)
    gid = lax.broadcasted_iota(jnp.int32, (rows, 1), 0) // blk
    sink = jnp.zeros((rows, 1), F32)
    for g in range(grp):
        sink = jnp.where(gid == g, sink_ref[kvh * grp + g], sink)
    mx = jnp.maximum(jnp.max(s_c, axis=-1, keepdims=True), sink)
    mx = jnp.maximum(mx, jnp.max(s_p, axis=-1, keepdims=True))
    mx = jnp.maximum(mx, jnp.max(s_o, axis=-1, keepdims=True))
    mx = jnp.maximum(mx, jnp.max(s_n, axis=-1, keepdims=True))
    p_c = jnp.exp(s_c - mx)
    p_p = jnp.exp(s_p - mx)
    p_o = jnp.exp(s_o - mx)
    p_n = jnp.exp(s_n - mx)
    den = (jnp.exp(sink - mx) + jnp.sum(p_c, axis=-1, keepdims=True)
           + jnp.sum(p_p, axis=-1, keepdims=True) + jnp.sum(p_o, axis=-1, keepdims=True)
           + jnp.sum(p_n, axis=-1, keepdims=True))
    o = (jnp.dot(p_c.astype(BF16), vc_ref[...], preferred_element_type=F32)
         + jnp.dot(p_p.astype(BF16), vp_ref[...], preferred_element_type=F32)
         + jnp.dot(p_o.astype(BF16), vo_ref[...], preferred_element_type=F32)
         + jnp.dot(p_n.astype(BF16), vn_ref[...], preferred_element_type=F32))
    o = o / den
    for g in range(grp):
        o_ref[:, g * hd:(g + 1) * hd] = o[g * blk:(g + 1) * blk, :].astype(o_ref.dtype)


def _attention(qkv, sinks, *, n_lat, n_ctx, n_heads):
    m = qkv.shape[0]
    blk = ATTN_BLOCK
    hd = ATTN_HEAD_DIM
    kvh = ATTN_KV_HEADS
    grp = n_heads // kvh
    n_blocks = m // blk
    n_lat_blocks = n_lat // blk
    k_col = n_heads
    v_col = n_heads + kvh
    ctx_row = n_lat // n_ctx
    prv = lambda i: jnp.clip(i - 1, 0, n_lat_blocks - 1)
    own = lambda i: jnp.minimum(i, n_lat_blocks - 1)
    nxt = lambda i: jnp.minimum(i + 1, n_lat_blocks - 1)
    return pl.pallas_call(
        functools.partial(_attn_kernel, n_lat_blocks=n_lat_blocks),
        out_shape=jax.ShapeDtypeStruct((m, n_heads * hd), BF16),
        grid=(kvh, n_blocks),
        in_specs=[
            pl.BlockSpec(memory_space=pltpu.SMEM),
            pl.BlockSpec((blk, grp * hd), lambda h, i: (i, h)),
            pl.BlockSpec((n_ctx, hd), lambda h, i: (ctx_row, k_col + h)),
            pl.BlockSpec((n_ctx, hd), lambda h, i: (ctx_row, v_col + h)),
            pl.BlockSpec((blk, hd), lambda h, i: (prv(i), k_col + h)),
            pl.BlockSpec((blk, hd), lambda h, i: (own(i), k_col + h)),
            pl.BlockSpec((blk, hd), lambda h, i: (nxt(i), k_col + h)),
            pl.BlockSpec((blk, hd), lambda h, i: (prv(i), v_col + h)),
            pl.BlockSpec((blk, hd), lambda h, i: (own(i), v_col + h)),
            pl.BlockSpec((blk, hd), lambda h, i: (nxt(i), v_col + h)),
        ],
        out_specs=pl.BlockSpec((blk, grp * hd), lambda h, i: (i, h)),
        compiler_params=_cparams(("parallel", "parallel")),
        name="window_attention",
    )(sinks, qkv, qkv, qkv, qkv, qkv, qkv, qkv, qkv, qkv)


def _rope_tables(n_lat, n_ctx):
    rows = n_lat // GRID_W
    half = ATTN_HEAD_DIM // 2
    row = jnp.repeat(jnp.arange(rows, dtype=F32), GRID_W)
    col = jnp.tile(jnp.arange(GRID_W, dtype=F32), rows)
    inv_freq = ROPE_THETA ** (-jnp.arange(0, half, 2, dtype=F32) / half)
    ar = row[:, None] * inv_freq[None, :]
    ac = col[:, None] * inv_freq[None, :]
    cos = jnp.concatenate([jnp.cos(ar), jnp.cos(ar), jnp.cos(ac), jnp.cos(ac)], axis=-1)
    sin = jnp.concatenate([-jnp.sin(ar), jnp.sin(ar), -jnp.sin(ac), jnp.sin(ac)], axis=-1)
    cos = jnp.concatenate([cos, jnp.ones((n_ctx, ATTN_HEAD_DIM), F32)], axis=0)
    sin = jnp.concatenate([sin, jnp.zeros((n_ctx, ATTN_HEAD_DIM), F32)], axis=0)
    return cos, sin


def _rows8(*rows):
    d = rows[0].shape[-1]
    pad = [jnp.zeros((d,), F32)] * (8 - len(rows))
    return jnp.stack(list(rows) + pad, axis=0)


def kernel(x, c, ctx, c_ctx, ada_w, ada_b, norm1_w, norm2_w, ssd_w_in, ssd_conv_w, ssd_conv_b,
           ssd_dt_bias_f, ssd_dt_bias_b, ssd_a_log_f, ssd_a_log_b, ssd_d, ssd_norm_w, ssd_w_out,
           attn_w_qkv, attn_q_gain, attn_k_gain, attn_sinks, attn_w_o,
           ffn_w_up, ffn_conv_w, ffn_conv_b, ffn_w_down):
    assert x.shape[0] == 1 and ctx.shape[0] == 1
    n_lat, d = x.shape[1], x.shape[2]
    n_ctx = ctx.shape[1]
    m = n_lat + n_ctx
    depth = ada_w.shape[0]
    assert m % ROW_TILE == 0 and n_lat % n_ctx == 0 and n_lat % SSD_CHUNK == 0
    assert n_ctx % SSD_CHUNK == 0 and n_lat % GRID_W == 0

    xs = jnp.concatenate([x[0], ctx[0]], axis=0)
    c2 = jnp.stack([c[0], c_ctx], axis=1)
    mods = _modulation(c2, ada_w, ada_b).reshape(depth, 2, 6, d)

    di = ssd_norm_w.shape[1]
    n_ssd_heads = ssd_d.shape[1]
    n_heads = attn_sinks.shape[1]
    q_dim = n_heads * ATTN_HEAD_DIM
    cos, sin = _rope_tables(n_lat, n_ctx)

    for i in range(depth):
        last = i == depth - 1
        j = i // 2
        sh1, sc1, g1, sh2, sc2, g2 = [mods[i, :, q, :] for q in range(6)]
        tab1 = _rows8(norm1_w[i], sc1[0], sh1[0], sc1[1], sh1[1])
        tab2 = _rows8(norm2_w[i], sc2[0], sh2[0], sc2[1], sh2[1])
        gate1 = _rows8(g1[0], g1[1])
        gate2 = _rows8(g2[0], g2[1])
        if i % 2 == 0:
            w_in = ssd_w_in[j].astype(BF16)
            n_main = 2 * di + 2 * SSD_GROUPS * SSD_STATE
            zx, dt = _norm_mod_matmul(xs, tab1, w_in[:, :n_main], n_lat=n_lat, tn=1024, kind="ssd",
                                      extra=(w_in[:, n_main:],))
            bias_fb = jnp.concatenate([ssd_dt_bias_f[j], ssd_dt_bias_b[j]]).reshape(1, -1)
            alog_fb = jnp.concatenate([ssd_a_log_f[j], ssd_a_log_b[j]]).reshape(1, -1)
            dskip = jnp.repeat(ssd_d[j], SSD_HEAD_DIM).reshape(1, di)
            gated = _ssd_mixer(zx, dt, ssd_conv_w[j], ssd_conv_b[j], bias_fb, alog_fb, dskip,
                               ssd_norm_w[j], n_lat=n_lat)
            xs = _matmul_residual(gated, ssd_w_out[j].astype(BF16), xs, gate1, n_lat=n_lat, tn=512)
        else:
            gains = _rows8(attn_q_gain[j], attn_k_gain[j])
            qkv = _norm_mod_matmul(xs, tab1, attn_w_qkv[j].astype(BF16), n_lat=n_lat, tn=512,
                                   kind="qkv",
                                   extra=(gains, cos, sin, q_dim // 512, ATTN_HEAD_DIM ** -0.5))
            o = _attention(qkv, attn_sinks[j], n_lat=n_lat, n_ctx=n_ctx, n_heads=n_heads)
            xs = _matmul_residual(o, attn_w_o[j].astype(BF16), xs, gate1, n_lat=n_lat, tn=512)
        u = _norm_mod_matmul(xs, tab2, ffn_w_up[i].astype(BF16), n_lat=n_lat, tn=1024, kind="plain")
        xs = _ffn_down(u, ffn_conv_w[i], ffn_conv_b[i], ffn_w_down[i].astype(BF16), xs, gate2,
                       n_lat=n_lat, m_out=(n_lat if last else m))
    return xs[None]
```

```python
import functools

import jax
import jax.numpy as jnp
from jax import lax
from jax.experimental import pallas as pl
from jax.experimental.pallas import tpu as pltpu

F32 = jnp.float32
BF16 = jnp.bfloat16

NORM_EPS = 1e-6
GRID_W = 64
ROPE_THETA = 10000.0
LOG2E = 1.4426950408889634

SSD_HEAD_DIM = 64
SSD_GROUPS = 8
SSD_STATE = 128
SSD_CHUNK = 128
ATTN_HEAD_DIM = 128
ATTN_KV_HEADS = 4
ATTN_BLOCK = 128

ROW_TILE = 768
HALO = 16
EPI_ROWS = 128
DOT_ROWS = 160
VMEM_LIMIT = 56 * 1024 * 1024


def _cparams(sem):
    return pltpu.CompilerParams(dimension_semantics=sem, vmem_limit_bytes=VMEM_LIMIT)


def _silu(v):
    return v / (1.0 + jnp.exp(-v))


def _softplus(v):
    return jnp.maximum(v, 0.0) + jnp.log(1.0 + jnp.exp(-jnp.abs(v)))


def _modulation_kernel(c_ref, w_ref, b_ref, o_ref, *, kc):
    d = c_ref.shape[0]
    tn = w_ref.shape[2]

    def body(k, acc):
        a0, a1 = acc
        k0 = pl.multiple_of(k * kc, kc)
        sv = _silu(c_ref[pl.ds(k0, kc), :])
        w = w_ref[0, pl.ds(k0, kc), :]
        a0 = a0 + jnp.sum(w * sv[:, 0:1], axis=0, keepdims=True)
        a1 = a1 + jnp.sum(w * sv[:, 1:2], axis=0, keepdims=True)
        return a0, a1

    z = jnp.zeros((1, tn), F32)
    a0, a1 = lax.fori_loop(0, d // kc, body, (z, z))
    bias = b_ref[0]
    o_ref[0, 0:1, :] = a0 + bias
    o_ref[0, 1:2, :] = a1 + bias


def _modulation(c2, ada_w, ada_b):
    depth, d, n = ada_w.shape
    tn = 1024
    return pl.pallas_call(
        functools.partial(_modulation_kernel, kc=256),
        out_shape=jax.ShapeDtypeStruct((depth, 2, n), F32),
        grid=(depth, n // tn),
        in_specs=[
            pl.BlockSpec((d, 2), lambda l, j: (0, 0)),
            pl.BlockSpec((1, d, tn), lambda l, j: (l, 0, j)),
            pl.BlockSpec((1, 1, tn), lambda l, j: (l, 0, j)),
        ],
        out_specs=pl.BlockSpec((1, 2, tn), lambda l, j: (l, 0, j)),
        compiler_params=_cparams(("parallel", "parallel")),
        name="adaln_modulation",
    )(c2, ada_w, ada_b.reshape(depth, 1, n))


def _mod_vectors(tab_ref):
    w = tab_ref[0:1, :]
    return (w * (1.0 + tab_ref[1:2, :]), tab_ref[2:3, :],
            w * (1.0 + tab_ref[3:4, :]), tab_ref[4:5, :])


def _norm_mod_store(x, vecs, is_ctx, a_ref, dst0):
    mult = jnp.where(is_ctx, vecs[2], vecs[0])
    shift = jnp.where(is_ctx, vecs[3], vecs[1])
    ms = jnp.mean(x * x, axis=-1, keepdims=True)
    a_ref[pl.ds(dst0, x.shape[0]), :] = (x * lax.rsqrt(ms + NORM_EPS) * mult + shift).astype(BF16)


def _fill_rows(x_ref, tab_ref, a_ref, i, *, n_lat, dst_off, halos=None):
    tm = x_ref.shape[0]
    sub = 128
    vecs = _mod_vectors(tab_ref)

    def body(r, carry):
        r0 = pl.multiple_of(r * sub, sub)
        _norm_mod_store(x_ref[pl.ds(r0, sub), :], vecs, (i * tm + r0) >= n_lat, a_ref,
                        pl.multiple_of(dst_off + r0, HALO))
        return carry

    lax.fori_loop(0, tm // sub, body, 0)
    if halos is not None:
        xp_ref, xn_ref, m_total = halos
        _norm_mod_store(xp_ref[...], vecs, (i * tm - HALO) >= n_lat, a_ref, 0)
        _norm_mod_store(xn_ref[...], vecs, ((i + 1) * tm) >= n_lat, a_ref, dst_off + tm)
        start = i * tm

        @pl.when((start == 0) | (start == n_lat))
        def _():
            a_ref[0:HALO, :] = jnp.zeros((HALO, a_ref.shape[1]), BF16)

        @pl.when((start + tm == n_lat) | (start + tm == m_total))
        def _():
            a_ref[dst_off + tm:dst_off + tm + HALO, :] = jnp.zeros((HALO, a_ref.shape[1]), BF16)


def _conv_rows(stage_ref, cw_ref, cb_ref, r0, nrows, masks=None):
    up = stage_ref[pl.ds(r0 - 1, nrows), :]
    v = stage_ref[pl.ds(r0, nrows), :]
    dn = stage_ref[pl.ds(r0 + 1, nrows), :]
    if masks is not None:
        up = jnp.where(masks[0], up, 0.0)
        dn = jnp.where(masks[1], dn, 0.0)
    return cw_ref[0:1, :] * up + cw_ref[1:2, :] * v + cw_ref[2:3, :] * dn + cb_ref[...]


def _interleave(tm, dot_rows, epi_rows, slab_fix):
    total = tm + 2 * HALO
    assert total % DOT_ROWS == 0 and tm % EPI_ROWS == 0
    n_epi = tm // EPI_ROWS
    done = 0
    for k in range(total // DOT_ROWS):
        dot_rows(k * DOT_ROWS, DOT_ROWS)
        while done < n_epi and HALO + (done + 1) * EPI_ROWS + 1 <= k * DOT_ROWS:
            epi_rows(done * EPI_ROWS, EPI_ROWS, None)
            done += 1
    while done < n_epi:
        epi_rows(done * EPI_ROWS, EPI_ROWS, None)
        done += 1
    slab_fix()


def _boundary_slab_fix(i, tm, n_lat, epi_rows):
    rb = n_lat % tm
    if rb:
        @pl.when(i == n_lat // tm)
        def _():
            rid = lax.broadcasted_iota(jnp.int32, (2 * HALO, 1), 0) + (rb - HALO)
            epi_rows(rb - HALO, 2 * HALO, (rid != rb, rid != rb - 1))


def _ssd_in_kernel(x_ref, xp_ref, xn_ref, tab_ref, w_ref, wdt_ref, cw_ref, cb_ref,
                   o_ref, dt_ref, a_ref, stage_ref, *, n_lat, m_total, n_z_tiles):
    i, j = pl.program_id(0), pl.program_id(1)
    tm = x_ref.shape[0]

    @pl.when(j == 0)
    def _():
        _fill_rows(x_ref, tab_ref, a_ref, i, n_lat=n_lat, dst_off=HALO,
                   halos=(xp_ref, xn_ref, m_total))
        dt_ref[...] = jnp.dot(a_ref[HALO:HALO + tm, :], wdt_ref[...], preferred_element_type=F32)

    def dot_rows(r0, nr):
        stage_ref[r0:r0 + nr, :] = jnp.dot(a_ref[r0:r0 + nr, :], w_ref[...],
                                           preferred_element_type=F32)

    @pl.when(j < n_z_tiles)
    def _():
        def epi(r, nr, masks):
            o_ref[r:r + nr, :] = _silu(stage_ref[HALO + r:HALO + r + nr, :]).astype(o_ref.dtype)

        _interleave(tm, dot_rows, epi, lambda: None)

    @pl.when(j >= n_z_tiles)
    def _():
        def epi(r, nr, masks):
            cv = _conv_rows(stage_ref, cw_ref, cb_ref, HALO + r, nr, masks)
            o_ref[r:r + nr, :] = _silu(cv).astype(o_ref.dtype)

        _interleave(tm, dot_rows, epi, lambda: _boundary_slab_fix(i, tm, n_lat, epi))


def _ssd_in_proj(x, tab, w_in, conv_w, conv_b, *, n_lat, d_inner):
    m, d = x.shape
    tm, tn = ROW_TILE, 1024
    n_main = 2 * d_inner + 2 * SSD_GROUPS * SSD_STATE
    ndt = w_in.shape[1] - n_main
    n_z_tiles = d_inner // tn
    hb = tm // HALO
    last_hb = m // HALO - 1
    cmap = lambda i, j: (0, jnp.maximum(j - n_z_tiles, 0))
    return pl.pallas_call(
        functools.partial(_ssd_in_kernel, n_lat=n_lat, m_total=m, n_z_tiles=n_z_tiles),
        out_shape=(jax.ShapeDtypeStruct((m, n_main), BF16), jax.ShapeDtypeStruct((m, ndt), F32)),
        grid=(m // tm, n_main // tn),
        in_specs=[
            pl.BlockSpec((tm, d), lambda i, j: (i, 0)),
            pl.BlockSpec((HALO, d), lambda i, j: (jnp.maximum(i * hb - 1, 0), 0)),
            pl.BlockSpec((HALO, d), lambda i, j: (jnp.minimum((i + 1) * hb, last_hb), 0)),
            pl.BlockSpec((8, d), lambda i, j: (0, 0)),
            pl.BlockSpec((d, tn), lambda i, j: (0, j)),
            pl.BlockSpec((d, ndt), lambda i, j: (0, n_main // ndt)),
            pl.BlockSpec((3, tn), cmap),
            pl.BlockSpec((1, tn), cmap),
        ],
        out_specs=(pl.BlockSpec((tm, tn), lambda i, j: (i, j)),
                   pl.BlockSpec((tm, ndt), lambda i, j: (i, 0))),
        scratch_shapes=[pltpu.VMEM((tm + 2 * HALO, d), BF16),
                        pltpu.VMEM((tm + 2 * HALO, tn), F32)],
        compiler_params=_cparams(("parallel", "arbitrary")),
        name="ssd_in_proj",
    )(x, x, x, tab, w_in, w_in, conv_w, conv_b.reshape(1, -1))


def _ffn_up_kernel(x_ref, xp_ref, xn_ref, tab_ref, wv_ref, wg_ref, cwv_ref, cwg_ref, cbv_ref,
                   cbg_ref, o_ref, a_ref, sv_ref, sg_ref, *, n_lat, m_total):
    i, j = pl.program_id(0), pl.program_id(1)
    tm = x_ref.shape[0]

    @pl.when(j == 0)
    def _():
        _fill_rows(x_ref, tab_ref, a_ref, i, n_lat=n_lat, dst_off=HALO,
                   halos=(xp_ref, xn_ref, m_total))

    def dot_rows(r0, nr):
        a = a_ref[r0:r0 + nr, :]
        sv_ref[r0:r0 + nr, :] = jnp.dot(a, wv_ref[...], preferred_element_type=F32)
        sg_ref[r0:r0 + nr, :] = jnp.dot(a, wg_ref[...], preferred_element_type=F32)

    def epi(r, nr, masks):
        val = _conv_rows(sv_ref, cwv_ref, cbv_ref, HALO + r, nr, masks)
        gat = _conv_rows(sg_ref, cwg_ref, cbg_ref, HALO + r, nr, masks)
        o_ref[r:r + nr, :] = (_silu(gat) * val).astype(o_ref.dtype)

    _interleave(tm, dot_rows, epi, lambda: _boundary_slab_fix(i, tm, n_lat, epi))


def _ffn_up(x, tab, w_up, conv_w, conv_b, *, n_lat):
    m, d = x.shape
    f = w_up.shape[1] // 2
    tm, tn = ROW_TILE, 512
    nj = f // tn
    hb = tm // HALO
    last_hb = m // HALO - 1
    cb2 = conv_b.reshape(1, -1)
    vmap = lambda i, j: (0, j)
    gmap = lambda i, j: (0, j + nj)
    return pl.pallas_call(
        functools.partial(_ffn_up_kernel, n_lat=n_lat, m_total=m),
        out_shape=jax.ShapeDtypeStruct((m, f), BF16),
        grid=(m // tm, nj),
        in_specs=[
            pl.BlockSpec((tm, d), lambda i, j: (i, 0)),
            pl.BlockSpec((HALO, d), lambda i, j: (jnp.maximum(i * hb - 1, 0), 0)),
            pl.BlockSpec((HALO, d), lambda i, j: (jnp.minimum((i + 1) * hb, last_hb), 0)),
            pl.BlockSpec((8, d), lambda i, j: (0, 0)),
            pl.BlockSpec((d, tn), vmap),
            pl.BlockSpec((d, tn), gmap),
            pl.BlockSpec((3, tn), vmap),
            pl.BlockSpec((3, tn), gmap),
            pl.BlockSpec((1, tn), vmap),
            pl.BlockSpec((1, tn), gmap),
        ],
        out_specs=pl.BlockSpec((tm, tn), lambda i, j: (i, j)),
        scratch_shapes=[pltpu.VMEM((tm + 2 * HALO, d), BF16),
                        pltpu.VMEM((tm + 2 * HALO, tn), F32),
                        pltpu.VMEM((tm + 2 * HALO, tn), F32)],
        compiler_params=_cparams(("parallel", "arbitrary")),
        name="convffn_up",
    )(x, x, x, tab, w_up, w_up, conv_w, conv_w, cb2, cb2)


def _rope_swap(t):
    lane = lax.broadcasted_iota(jnp.int32, t.shape, 1)
    first = (lane % 64) < 32
    return jnp.where(first, pltpu.roll(t, 96, 1), pltpu.roll(t, 32, 1))


def _qkv_kernel(x_ref, tab_ref, w_ref, gain_ref, cos_ref, sin_ref, o_ref, a_ref, *,
                n_lat, n_q_tiles, q_scale):
    i, j = pl.program_id(0), pl.program_id(1)
    hd = ATTN_HEAD_DIM

    @pl.when(j == 0)
    def _():
        _fill_rows(x_ref, tab_ref, a_ref, i, n_lat=n_lat, dst_off=0)

    acc = jnp.dot(a_ref[...], w_ref[...], preferred_element_type=F32)
    tn = acc.shape[1]

    @pl.when(j <= n_q_tiles)
    def _():
        is_q = j < n_q_tiles
        gain = jnp.where(is_q, gain_ref[0:1, :], gain_ref[1:2, :])
        post = jnp.where(is_q, q_scale, 1.0)
        cos = cos_ref[...]
        sin = sin_ref[...]
        for h in range(tn // hd):
            t = acc[:, h * hd:(h + 1) * hd]
            ms = jnp.mean(t * t, axis=-1, keepdims=True)
            t = t * lax.rsqrt(ms + NORM_EPS) * gain
            t = (t * cos + _rope_swap(t) * sin) * post
            o_ref[:, h * hd:(h + 1) * hd] = t.astype(o_ref.dtype)

    @pl.when(j > n_q_tiles)
    def _():
        o_ref[...] = acc.astype(o_ref.dtype)


def _qkv_proj(x, tab, w, gains, cos, sin, *, n_lat, q_dim, q_scale):
    m, d = x.shape
    n = w.shape[1]
    tm, tn = ROW_TILE, 512
    assert (n - q_dim) == 2 * tn
    return pl.pallas_call(
        functools.partial(_qkv_kernel, n_lat=n_lat, n_q_tiles=q_dim // tn, q_scale=q_scale),
        out_shape=jax.ShapeDtypeStruct((m, n), BF16),
        grid=(m // tm, n // tn),
        in_specs=[pl.BlockSpec((tm, d), lambda i, j: (i, 0)),
                  pl.BlockSpec((8, d), lambda i, j: (0, 0)),
                  pl.BlockSpec((d, tn), lambda i, j: (0, j)),
                  pl.BlockSpec((8, ATTN_HEAD_DIM), lambda i, j: (0, 0)),
                  pl.BlockSpec((tm, ATTN_HEAD_DIM), lambda i, j: (i, 0)),
                  pl.BlockSpec((tm, ATTN_HEAD_DIM), lambda i, j: (i, 0))],
        out_specs=pl.BlockSpec((tm, tn), lambda i, j: (i, j)),
        scratch_shapes=[pltpu.VMEM((tm, d), BF16)],
        compiler_params=_cparams(("parallel", "arbitrary")),
        name="attn_qkv_proj",
    )(x, tab, w, gains, cos, sin)


def _mm_res_kernel(a_ref, w_ref, x_ref, g_ref, o_ref, *, n_lat):
    i = pl.program_id(0)
    tm = a_ref.shape[0]
    acc = jnp.dot(a_ref[...], w_ref[...], preferred_element_type=F32)
    rows = i * tm + lax.broadcasted_iota(jnp.int32, (tm, 1), 0)
    gate = jnp.where(rows >= n_lat, g_ref[1:2, :], g_ref[0:1, :])
    o_ref[...] = x_ref[...] + gate * acc


def _matmul_residual(a, w, x, gates, *, n_lat, tn, m_out=None):
    m, k = a.shape
    n = w.shape[1]
    tm = ROW_TILE
    return pl.pallas_call(
        functools.partial(_mm_res_kernel, n_lat=n_lat),
        out_shape=jax.ShapeDtypeStruct((m if m_out is None else m_out, n), F32),
        grid=(m // tm, n // tn),
        in_specs=[pl.BlockSpec((tm, k), lambda i, j: (i, 0)),
                  pl.BlockSpec((k, tn), lambda i, j: (0, j)),
                  pl.BlockSpec((tm, tn), lambda i, j: (i, j)),
                  pl.BlockSpec((8, tn), lambda i, j: (0, j))],
        out_specs=pl.BlockSpec((tm, tn), lambda i, j: (i, j)),
        compiler_params=_cparams(("parallel", "parallel")),
        name="matmul_gated_residual",
    )(a, w, x, gates)


def _ssd_scan_chunk(xs_ref, bc_ref, dt_raw, bias_ref, alog_ref, y_ref, state_ref, col_ref, rowt_ref,
                    *, reverse, hoff, first):
    L = SSD_CHUNK
    P = SSD_HEAD_DIM
    G = SSD_GROUPS
    N = SSD_STATE
    H = xs_ref.shape[1] // P
    E = H // G
    EP = E * P
    PAIR = 2 * P
    assert PAIR == 128 and E % 2 == 0

    @pl.when(first)
    def _():
        state_ref[...] = jnp.zeros_like(state_ref)

    dt = _softplus(dt_raw + bias_ref[...])
    da2 = (-jnp.exp(alog_ref[...]) * dt) * LOG2E
    li = lax.broadcasted_iota(jnp.int32, (L, L), 0)
    si = lax.broadcasted_iota(jnp.int32, (L, L), 1)
    keep = (si >= li) if reverse else (si <= li)
    acs2 = jnp.dot(keep.astype(F32), da2, preferred_element_type=F32,
                   precision=lax.Precision.HIGHEST)
    edge = 0 if reverse else L - 1
    a2t = acs2.T
    r2t = a2t - jnp.log2(dt.T)
    edge_b = jnp.broadcast_to(a2t[:, edge:edge + 1], (128, L))
    rowt_ref[0] = r2t
    rowt_ref[1] = jnp.exp2(edge_b - r2t)
    rowt_ref[2] = jnp.exp2(edge_b)
    for g in range(G):
        lo = hoff + g * E
        col_ref[g, :, 0:E] = acs2[:, lo:lo + E]

    lane_lo = lax.broadcasted_iota(jnp.int32, (L, PAIR), 1) < P
    lane_lo1 = lax.broadcasted_iota(jnp.int32, (1, PAIR), 1) < P
    nt = (((1,), (1,)), ((), ()))

    def group(g, carry):
        b_off = pl.multiple_of(g * N, N)
        c_off = pl.multiple_of((G + g) * N, N)
        bg = bc_ref[:, pl.ds(b_off, N)]
        cg = bc_ref[:, pl.ds(c_off, N)]
        cb = lax.dot_general(cg, bg, nt, preferred_element_type=F32)
        bgt = bg.astype(F32).T
        h_in = state_ref[g]
        y_off = jnp.dot(cg, h_in.astype(BF16), preferred_element_type=F32)
        cols = col_ref[g]
        hrow = pl.multiple_of(hoff + g * E, E)
        rows_r = rowt_ref[0, pl.ds(hrow, E), :]
        rows_f = rowt_ref[1, pl.ds(hrow, E), :]
        rows_c = rowt_ref[2, pl.ds(hrow, E), :]
        x_off = pl.multiple_of(g * EP, EP)
        for pr in range(E // 2):
            e1, e2 = 2 * pr, 2 * pr + 1
            colb1 = jnp.broadcast_to(cols[:, e1:e1 + 1], (L, L))
            colb2 = jnp.broadcast_to(cols[:, e2:e2 + 1], (L, L))
            w1 = (cb * jnp.exp2(jnp.where(keep, colb1 - rows_r[e1:e1 + 1, :], -jnp.inf))).astype(BF16)
            w2 = (cb * jnp.exp2(jnp.where(keep, colb2 - rows_r[e2:e2 + 1, :], -jnp.inf))).astype(BF16)
            s1 = (bgt * rows_f[e1:e1 + 1, :]).astype(BF16)
            s2 = (bgt * rows_f[e2:e2 + 1, :]).astype(BF16)
            lhs = jnp.concatenate([jnp.concatenate([w1, w2], axis=1),
                                   jnp.concatenate([s1, s2], axis=1)], axis=0)
            xp = xs_ref[:, pl.ds(x_off + pr * PAIR, PAIR)]
            zero = jnp.zeros_like(xp)
            xbd = jnp.concatenate([jnp.where(lane_lo, xp, zero), jnp.where(lane_lo, zero, xp)],
                                  axis=0)
            res = jnp.dot(lhs, xbd, preferred_element_type=F32)
            e_in = jnp.exp2(jnp.where(lane_lo, colb1, colb2))
            y_ref[:, pl.ds(x_off + pr * PAIR, PAIR)] = (
                res[0:L] + y_off[:, pr * PAIR:(pr + 1) * PAIR] * e_in)
            cd = jnp.where(lane_lo1, rows_c[e1:e1 + 1, :], rows_c[e2:e2 + 1, :])
            state_ref[g, :, pr * PAIR:(pr + 1) * PAIR] = (
                cd * h_in[:, pr * PAIR:(pr + 1) * PAIR] + res[L:2 * L])
        return carry

    lax.fori_loop(0, G, group, 0, unroll=2)


def _ssd_fwd_kernel(xs_ref, bc_ref, dt_ref, bias_ref, alog_ref, y_ref,
                    state_ref, col_ref, rowt_ref, yacc_ref):
    t = pl.program_id(0)
    _ssd_scan_chunk(xs_ref, bc_ref, dt_ref[...], bias_ref, alog_ref, yacc_ref, state_ref, col_ref,
                    rowt_ref, reverse=False, hoff=0, first=(t == 0))
    y_ref[...] = yacc_ref[...].astype(y_ref.dtype)


def _ssd_bwd_kernel(xs_ref, bc_ref, dt_ref, yf_ref, sz_ref, dskip_ref, nw_ref, bias_ref, alog_ref,
                    g_ref, state_ref, col_ref, rowt_ref, yacc_ref, *, hoff):
    t = pl.program_id(0)
    _ssd_scan_chunk(xs_ref, bc_ref, dt_ref[...], bias_ref, alog_ref, yacc_ref, state_ref, col_ref,
                    rowt_ref, reverse=True, hoff=hoff, first=(t == 0))
    y = (yacc_ref[...] + yf_ref[...].astype(F32) + xs_ref[...].astype(F32) * dskip_ref[...])
    y = y * sz_ref[...].astype(F32)
    ms = jnp.mean(y * y, axis=-1, keepdims=True)
    g_ref[...] = (y * lax.rsqrt(ms + NORM_EPS) * nw_ref[...]).astype(g_ref.dtype)


def _ssd_mixer(zx, dt, bias_fb, alog_fb, dskip, norm_w, *, n_lat):
    m = zx.shape[0]
    di = norm_w.shape[0]
    gn2 = 2 * SSD_GROUPS * SSD_STATE
    L = SSD_CHUNK
    n_chunks = m // L
    n_lat_chunks = n_lat // L
    x_cb = 1
    bc_cb = (2 * di) // gn2
    cf = lambda t: (t + n_lat_chunks) % n_chunks
    scratch = [pltpu.VMEM((SSD_GROUPS, SSD_STATE, di // SSD_GROUPS), F32),
               pltpu.VMEM((SSD_GROUPS, L, 128), F32),
               pltpu.VMEM((3, 128, L), F32),
               pltpu.VMEM((L, di), F32)]
    vec = lambda width: pl.BlockSpec((1, width), lambda t: (0, 0))
    yf = pl.pallas_call(
        _ssd_fwd_kernel,
        out_shape=jax.ShapeDtypeStruct((m, di), BF16),
        grid=(n_chunks,),
        in_specs=[
            pl.BlockSpec((L, di), lambda t: (cf(t), x_cb)),
            pl.BlockSpec((L, gn2), lambda t: (cf(t), bc_cb)),
            pl.BlockSpec((L, 128), lambda t: (cf(t), 0)),
            vec(128), vec(128),
        ],
        out_specs=pl.BlockSpec((L, di), lambda t: (cf(t), 0)),
        scratch_shapes=scratch,
        compiler_params=_cparams(("arbitrary",)),
        name="ssd_scan_fwd",
    )(zx, zx, dt, bias_fb, alog_fb)

    cbk = lambda t: n_chunks - 1 - t
    return pl.pallas_call(
        functools.partial(_ssd_bwd_kernel, hoff=di // SSD_HEAD_DIM),
        out_shape=jax.ShapeDtypeStruct((m, di), BF16),
        grid=(n_chunks,),
        in_specs=[
            pl.BlockSpec((L, di), lambda t: (cbk(t), x_cb)),
            pl.BlockSpec((L, gn2), lambda t: (cbk(t), bc_cb)),
            pl.BlockSpec((L, 128), lambda t: (cbk(t), 0)),
            pl.BlockSpec((L, di), lambda t: (cbk(t), 0)),
            pl.BlockSpec((L, di), lambda t: (cbk(t), 0)),
            vec(di), vec(di), vec(128), vec(128),
        ],
        out_specs=pl.BlockSpec((L, di), lambda t: (cbk(t), 0)),
        scratch_shapes=scratch,
        compiler_params=_cparams(("arbitrary",)),
        name="ssd_scan_bwd_finish",
    )(zx, zx, dt, yf, zx, dskip, norm_w.reshape(1, di), bias_fb, alog_fb)


def _attn_kernel(sink_ref, q_ref, kc_ref, vc_ref, kp_ref, ko_ref, kn_ref, vp_ref, vo_ref, vn_ref,
                 o_ref, *, n_lat_blocks, n_kv):
    qb = pl.program_id(0)
    blk = ATTN_BLOCK
    hd = ATTN_HEAD_DIM
    grp = q_ref.shape[1] // (hd * n_kv)
    rows = grp * blk
    n_ctx = kc_ref.shape[0]
    qi = lax.broadcasted_iota(jnp.int32, (rows, blk), 0) % blk
    kj = lax.broadcasted_iota(jnp.int32, (rows, blk), 1)
    is_lat = qb < n_lat_blocks
    keep_p = (is_lat & (qb > 0)) & (kj >= qi)
    keep_n = (is_lat & (qb < n_lat_blocks - 1)) & (kj <= qi)
    gid = lax.broadcasted_iota(jnp.int32, (rows, 1), 0) // blk
    nt = (((1,), (1,)), ((), ()))
    neg = -jnp.inf
    for kv in range(n_kv):
        ks = slice(kv * hd, (kv + 1) * hd)
        q = jnp.concatenate([q_ref[:, (kv * grp + g) * hd:(kv * grp + g + 1) * hd]
                             for g in range(grp)], axis=0)
        k_all = jnp.concatenate([kc_ref[:, ks], kp_ref[:, ks], ko_ref[:, ks], kn_ref[:, ks]], axis=0)
        v_all = jnp.concatenate([vc_ref[:, ks], vp_ref[:, ks], vo_ref[:, ks], vn_ref[:, ks]], axis=0)
        s = lax.dot_general(q, k_all, nt, preferred_element_type=F32)
        s_c = [s[:, c0:c0 + blk] for c0 in range(0, n_ctx, blk)]
        s_p = jnp.where(keep_p, s[:, n_ctx:n_ctx + blk], neg)
        s_o = jnp.where(is_lat, s[:, n_ctx + blk:n_ctx + 2 * blk], neg)
        s_n = jnp.where(keep_n, s[:, n_ctx + 2 * blk:n_ctx + 3 * blk], neg)
        tiles = s_c + [s_p, s_o, s_n]
        sink = jnp.zeros((rows, 1), F32)
        for g in range(grp):
            sink = jnp.where(gid == g, sink_ref[kv * grp + g] * LOG2E, sink)
        mx_el = tiles[0]
        for t in tiles[1:]:
            mx_el = jnp.maximum(mx_el, t)
        mx = jnp.maximum(jnp.max(mx_el, axis=-1, keepdims=True), sink)
        ps = [jnp.exp2(t - mx) for t in tiles]
        sum_el = ps[0]
        for t in ps[1:]:
            sum_el = sum_el + t
        den = jnp.exp2(sink - mx) + jnp.sum(sum_el, axis=-1, keepdims=True)
        p = jnp.concatenate([t.astype(BF16) for t in ps], axis=1)
        o = jnp.dot(p, v_all, preferred_element_type=F32) / den
        for g in range(grp):
            c0 = (kv * grp + g) * hd
            o_ref[:, c0:c0 + hd] = o[g * blk:(g + 1) * blk, :].astype(o_ref.dtype)


def _attention(qkv, sinks, *, n_lat, n_ctx, n_heads):
    m = qkv.shape[0]
    blk = ATTN_BLOCK
    hd = ATTN_HEAD_DIM
    n_kv = ATTN_KV_HEADS
    q_dim = n_heads * hd
    kv_dim = n_kv * hd
    n_blocks = m // blk
    n_lat_blocks = n_lat // blk
    assert q_dim % kv_dim == 0
    k_col = q_dim // kv_dim
    v_col = k_col + 1
    ctx_row = n_lat // n_ctx
    prv = lambda i: jnp.clip(i - 1, 0, n_lat_blocks - 1)
    own = lambda i: jnp.minimum(i, n_lat_blocks - 1)
    nxt = lambda i: jnp.minimum(i + 1, n_lat_blocks - 1)
    return pl.pallas_call(
        functools.partial(_attn_kernel, n_lat_blocks=n_lat_blocks, n_kv=n_kv),
        out_shape=jax.ShapeDtypeStruct((m, q_dim), BF16),
        grid=(n_blocks,),
        in_specs=[
            pl.BlockSpec(memory_space=pltpu.SMEM),
            pl.BlockSpec((blk, q_dim), lambda i: (i, 0)),
            pl.BlockSpec((n_ctx, kv_dim), lambda i: (ctx_row, k_col)),
            pl.BlockSpec((n_ctx, kv_dim), lambda i: (ctx_row, v_col)),
            pl.BlockSpec((blk, kv_dim), lambda i: (prv(i), k_col)),
            pl.BlockSpec((blk, kv_dim), lambda i: (own(i), k_col)),
            pl.BlockSpec((blk, kv_dim), lambda i: (nxt(i), k_col)),
            pl.BlockSpec((blk, kv_dim), lambda i: (prv(i), v_col)),
            pl.BlockSpec((blk, kv_dim), lambda i: (own(i), v_col)),
            pl.BlockSpec((blk, kv_dim), lambda i: (nxt(i), v_col)),
        ],
        out_specs=pl.BlockSpec((blk, q_dim), lambda i: (i, 0)),
        compiler_params=_cparams(("parallel",)),
        name="window_attention",
    )(sinks, qkv, qkv, qkv, qkv, qkv, qkv, qkv, qkv, qkv)


def _rope_tables(n_lat, n_ctx):
    rows = n_lat // GRID_W
    half = ATTN_HEAD_DIM // 2
    row = jnp.repeat(jnp.arange(rows, dtype=F32), GRID_W)
    col = jnp.tile(jnp.arange(GRID_W, dtype=F32), rows)
    inv_freq = ROPE_THETA ** (-jnp.arange(0, half, 2, dtype=F32) / half)
    ar = row[:, None] * inv_freq[None, :]
    ac = col[:, None] * inv_freq[None, :]
    cos = jnp.concatenate([jnp.cos(ar), jnp.cos(ar), jnp.cos(ac), jnp.cos(ac)], axis=-1)
    sin = jnp.concatenate([-jnp.sin(ar), jnp.sin(ar), -jnp.sin(ac), jnp.sin(ac)], axis=-1)
    cos = jnp.concatenate([cos, jnp.ones((n_ctx, ATTN_HEAD_DIM), F32)], axis=0)
    sin = jnp.concatenate([sin, jnp.zeros((n_ctx, ATTN_HEAD_DIM), F32)], axis=0)
    return cos, sin


def _rows8(*rows):
    d = rows[0].shape[-1]
    pad = [jnp.zeros((d,), F32)] * (8 - len(rows))
    return jnp.stack(list(rows) + pad, axis=0)


def kernel(x, c, ctx, c_ctx, ada_w, ada_b, norm1_w, norm2_w, ssd_w_in, ssd_conv_w, ssd_conv_b,
           ssd_dt_bias_f, ssd_dt_bias_b, ssd_a_log_f, ssd_a_log_b, ssd_d, ssd_norm_w, ssd_w_out,
           attn_w_qkv, attn_q_gain, attn_k_gain, attn_sinks, attn_w_o,
           ffn_w_up, ffn_conv_w, ffn_conv_b, ffn_w_down):
    assert x.shape[0] == 1 and ctx.shape[0] == 1
    n_lat, d = x.shape[1], x.shape[2]
    n_ctx = ctx.shape[1]
    m = n_lat + n_ctx
    depth = ada_w.shape[0]
    assert m % ROW_TILE == 0 and n_lat % n_ctx == 0 and n_lat % SSD_CHUNK == 0
    assert n_ctx % SSD_CHUNK == 0 and n_lat % GRID_W == 0

    xs = jnp.concatenate([x[0], ctx[0]], axis=0)
    c2 = jnp.stack([c[0], c_ctx], axis=1)
    mods = _modulation(c2, ada_w, ada_b).reshape(depth, 2, 6, d)

    di = ssd_norm_w.shape[1]
    n_heads = attn_sinks.shape[1]
    q_dim = n_heads * ATTN_HEAD_DIM
    cos, sin = _rope_tables(n_lat, n_ctx)

    for i in range(depth):
        last = i == depth - 1
        j = i // 2
        sh1, sc1, g1, sh2, sc2, g2 = [mods[i, :, q, :] for q in range(6)]
        tab1 = _rows8(norm1_w[i], sc1[0], sh1[0], sc1[1], sh1[1])
        tab2 = _rows8(norm2_w[i], sc2[0], sh2[0], sc2[1], sh2[1])
        gate1 = _rows8(g1[0], g1[1])
        gate2 = _rows8(g2[0], g2[1])
        if i % 2 == 0:
            zx, dt = _ssd_in_proj(xs, tab1, ssd_w_in[j].astype(BF16), ssd_conv_w[j], ssd_conv_b[j],
                                  n_lat=n_lat, d_inner=di)
            bias_fb = jnp.concatenate([ssd_dt_bias_f[j], ssd_dt_bias_b[j]]).reshape(1, -1)
            alog_fb = jnp.concatenate([ssd_a_log_f[j], ssd_a_log_b[j]]).reshape(1, -1)
            dskip = jnp.repeat(ssd_d[j], SSD_HEAD_DIM).reshape(1, di)
            gated = _ssd_mixer(zx, dt, bias_fb, alog_fb, dskip, ssd_norm_w[j], n_lat=n_lat)
            xs = _matmul_residual(gated, ssd_w_out[j].astype(BF16), xs, gate1, n_lat=n_lat, tn=512)
        else:
            gains = _rows8(attn_q_gain[j], attn_k_gain[j])
            qkv = _qkv_proj(xs, tab1, attn_w_qkv[j].astype(BF16), gains, cos, sin, n_lat=n_lat,
                            q_dim=q_dim, q_scale=ATTN_HEAD_DIM ** -0.5 * LOG2E)
            o = _attention(qkv, attn_sinks[j], n_lat=n_lat, n_ctx=n_ctx, n_heads=n_heads)
            xs = _matmul_residual(o, attn_w_o[j].astype(BF16), xs, gate1, n_lat=n_lat, tn=512)
        a = _ffn_up(xs, tab2, ffn_w_up[i].astype(BF16), ffn_conv_w[i], ffn_conv_b[i], n_lat=n_lat)
        xs = _matmul_residual(a, ffn_w_down[i].astype(BF16), xs, gate2, n_lat=n_lat, tn=512,
                              m_out=(n_lat if last else None))
    return xs[None]
```

```python
import functools

import jax
import jax.numpy as jnp
from jax import lax
from jax.experimental import pallas as pl
from jax.experimental.pallas import tpu as pltpu

F32 = jnp.float32
BF16 = jnp.bfloat16

NORM_EPS = 1e-6
GRID_W = 64
ROPE_THETA = 10000.0
LOG2E = 1.4426950408889634

SSD_HEAD_DIM = 64
SSD_GROUPS = 8
SSD_STATE = 128
SSD_CHUNK = 128
ATTN_HEAD_DIM = 128
ATTN_KV_HEADS = 4
ATTN_BLOCK = 128

ROW_TILE = 768
HALO = 16
EPI_ROWS = 128
QKV_ROWS = 192
DOT_ROWS = 160
VMEM_LIMIT = 56 * 1024 * 1024


def _cparams(sem):
    return pltpu.CompilerParams(dimension_semantics=sem, vmem_limit_bytes=VMEM_LIMIT)


def _silu(v):
    return v / (1.0 + jnp.exp(-v))


def _softplus(v):
    return jnp.maximum(v, 0.0) + jnp.log(1.0 + jnp.exp(-jnp.abs(v)))


def _modulation_kernel(c_ref, w_ref, b_ref, o_ref, *, kc):
    d = c_ref.shape[0]
    tn = w_ref.shape[2]

    def body(k, acc):
        a0, a1 = acc
        k0 = pl.multiple_of(k * kc, kc)
        sv = _silu(c_ref[pl.ds(k0, kc), :])
        w = w_ref[0, pl.ds(k0, kc), :]
        a0 = a0 + jnp.sum(w * sv[:, 0:1], axis=0, keepdims=True)
        a1 = a1 + jnp.sum(w * sv[:, 1:2], axis=0, keepdims=True)
        return a0, a1

    z = jnp.zeros((1, tn), F32)
    a0, a1 = lax.fori_loop(0, d // kc, body, (z, z))
    bias = b_ref[0]
    o_ref[0, 0:1, :] = a0 + bias
    o_ref[0, 1:2, :] = a1 + bias


def _modulation(c2, ada_w, ada_b):
    depth, d, n = ada_w.shape
    tn = 1024
    return pl.pallas_call(
        functools.partial(_modulation_kernel, kc=256),
        out_shape=jax.ShapeDtypeStruct((depth, 2, n), F32),
        grid=(depth, n // tn),
        in_specs=[
            pl.BlockSpec((d, 2), lambda l, j: (0, 0)),
            pl.BlockSpec((1, d, tn), lambda l, j: (l, 0, j)),
            pl.BlockSpec((1, 1, tn), lambda l, j: (l, 0, j)),
        ],
        out_specs=pl.BlockSpec((1, 2, tn), lambda l, j: (l, 0, j)),
        compiler_params=_cparams(("parallel", "parallel")),
        name="adaln_modulation",
    )(c2, ada_w, ada_b.reshape(depth, 1, n))


def _mod_vectors(tab_ref):
    w = tab_ref[0:1, :]
    return (w * (1.0 + tab_ref[1:2, :]), tab_ref[2:3, :],
            w * (1.0 + tab_ref[3:4, :]), tab_ref[4:5, :])


def _norm_mod_store(x, vecs, is_ctx, a_ref, dst0):
    mult = jnp.where(is_ctx, vecs[2], vecs[0])
    shift = jnp.where(is_ctx, vecs[3], vecs[1])
    ms = jnp.mean(x * x, axis=-1, keepdims=True)
    a_ref[pl.ds(dst0, x.shape[0]), :] = (x * lax.rsqrt(ms + NORM_EPS) * mult + shift).astype(BF16)


def _fill_rows(x_ref, tab_ref, a_ref, i, *, n_lat, dst_off, halos=None):
    tm = x_ref.shape[0]
    sub = 128
    vecs = _mod_vectors(tab_ref)

    def body(r, carry):
        r0 = pl.multiple_of(r * sub, sub)
        _norm_mod_store(x_ref[pl.ds(r0, sub), :], vecs, (i * tm + r0) >= n_lat, a_ref,
                        pl.multiple_of(dst_off + r0, HALO))
        return carry

    lax.fori_loop(0, tm // sub, body, 0)
    if halos is not None:
        xp_ref, xn_ref, m_total = halos
        _norm_mod_store(xp_ref[...], vecs, (i * tm - HALO) >= n_lat, a_ref, 0)
        _norm_mod_store(xn_ref[...], vecs, ((i + 1) * tm) >= n_lat, a_ref, dst_off + tm)
        start = i * tm

        @pl.when((start == 0) | (start == n_lat))
        def _():
            a_ref[0:HALO, :] = jnp.zeros((HALO, a_ref.shape[1]), BF16)

        @pl.when((start + tm == n_lat) | (start + tm == m_total))
        def _():
            a_ref[dst_off + tm:dst_off + tm + HALO, :] = jnp.zeros((HALO, a_ref.shape[1]), BF16)


def _conv_rows(stage_ref, cw_ref, cb_ref, r0, nrows, masks=None):
    up = stage_ref[pl.ds(r0 - 1, nrows), :]
    v = stage_ref[pl.ds(r0, nrows), :]
    dn = stage_ref[pl.ds(r0 + 1, nrows), :]
    if masks is not None:
        up = jnp.where(masks[0], up, 0.0)
        dn = jnp.where(masks[1], dn, 0.0)
    return cw_ref[0:1, :] * up + cw_ref[1:2, :] * v + cw_ref[2:3, :] * dn + cb_ref[...]


def _interleave(tm, dot_rows, epi_rows, slab_fix):
    total = tm + 2 * HALO
    assert total % DOT_ROWS == 0 and tm % EPI_ROWS == 0
    n_epi = tm // EPI_ROWS
    done = 0
    for k in range(total // DOT_ROWS):
        dot_rows(k * DOT_ROWS, DOT_ROWS)
        while done < n_epi and HALO + (done + 1) * EPI_ROWS + 1 <= k * DOT_ROWS:
            epi_rows(done * EPI_ROWS, EPI_ROWS, None)
            done += 1
    while done < n_epi:
        epi_rows(done * EPI_ROWS, EPI_ROWS, None)
        done += 1
    slab_fix()


def _boundary_slab_fix(i, tm, n_lat, epi_rows):
    rb = n_lat % tm
    if rb:
        @pl.when(i == n_lat // tm)
        def _():
            rid = lax.broadcasted_iota(jnp.int32, (2 * HALO, 1), 0) + (rb - HALO)
            epi_rows(rb - HALO, 2 * HALO, (rid != rb, rid != rb - 1))


def _ssd_in_kernel(x_ref, xp_ref, xn_ref, tab_ref, w_ref, wdt_ref, cw_ref, cb_ref,
                   o_ref, dt_ref, a_ref, stage_ref, *, n_lat, m_total, n_z_tiles):
    i, j = pl.program_id(0), pl.program_id(1)
    tm = x_ref.shape[0]

    @pl.when(j == 0)
    def _():
        _fill_rows(x_ref, tab_ref, a_ref, i, n_lat=n_lat, dst_off=HALO,
                   halos=(xp_ref, xn_ref, m_total))
        dt_ref[...] = jnp.dot(a_ref[HALO:HALO + tm, :], wdt_ref[...], preferred_element_type=F32)

    def dot_rows(r0, nr):
        stage_ref[r0:r0 + nr, :] = jnp.dot(a_ref[r0:r0 + nr, :], w_ref[...],
                                           preferred_element_type=F32)

    @pl.when(j < n_z_tiles)
    def _():
        def epi(r, nr, masks):
            o_ref[r:r + nr, :] = _silu(stage_ref[HALO + r:HALO + r + nr, :]).astype(o_ref.dtype)

        _interleave(tm, dot_rows, epi, lambda: None)

    @pl.when(j >= n_z_tiles)
    def _():
        def epi(r, nr, masks):
            cv = _conv_rows(stage_ref, cw_ref, cb_ref, HALO + r, nr, masks)
            o_ref[r:r + nr, :] = _silu(cv).astype(o_ref.dtype)

        _interleave(tm, dot_rows, epi, lambda: _boundary_slab_fix(i, tm, n_lat, epi))


def _ssd_in_proj(x, tab, w_in, conv_w, conv_b, layer, *, n_lat, d_inner):
    m, d = x.shape
    tm, tn = ROW_TILE, 1024
    n_main = 2 * d_inner + 2 * SSD_GROUPS * SSD_STATE
    ndt = w_in.shape[2] - n_main
    n_z_tiles = d_inner // tn
    hb = tm // HALO
    last_hb = m // HALO - 1
    cmap = lambda i, j: (layer, 0, jnp.maximum(j - n_z_tiles, 0))
    return pl.pallas_call(
        functools.partial(_ssd_in_kernel, n_lat=n_lat, m_total=m, n_z_tiles=n_z_tiles),
        out_shape=(jax.ShapeDtypeStruct((m, n_main), BF16), jax.ShapeDtypeStruct((m, ndt), F32)),
        grid=(m // tm, n_main // tn),
        in_specs=[
            pl.BlockSpec((tm, d), lambda i, j: (i, 0)),
            pl.BlockSpec((HALO, d), lambda i, j: (jnp.maximum(i * hb - 1, 0), 0)),
            pl.BlockSpec((HALO, d), lambda i, j: (jnp.minimum((i + 1) * hb, last_hb), 0)),
            pl.BlockSpec((8, d), lambda i, j: (0, 0)),
            pl.BlockSpec((None, d, tn), lambda i, j: (layer, 0, j)),
            pl.BlockSpec((None, d, ndt), lambda i, j: (layer, 0, n_main // ndt)),
            pl.BlockSpec((None, 3, tn), cmap),
            pl.BlockSpec((None, 1, tn), cmap),
        ],
        out_specs=(pl.BlockSpec((tm, tn), lambda i, j: (i, j)),
                   pl.BlockSpec((tm, ndt), lambda i, j: (i, 0))),
        scratch_shapes=[pltpu.VMEM((tm + 2 * HALO, d), BF16),
                        pltpu.VMEM((tm + 2 * HALO, tn), F32)],
        compiler_params=_cparams(("parallel", "arbitrary")),
        name="ssd_in_proj",
    )(x, x, x, tab, w_in, w_in, conv_w, conv_b)


def _ffn_up_kernel(x_ref, xp_ref, xn_ref, tab_ref, wv_ref, wg_ref, cwv_ref, cwg_ref, cbv_ref,
                   cbg_ref, o_ref, a_ref, sv_ref, sg_ref, *, n_lat, m_total):
    i, j = pl.program_id(0), pl.program_id(1)
    tm = x_ref.shape[0]

    @pl.when(j == 0)
    def _():
        _fill_rows(x_ref, tab_ref, a_ref, i, n_lat=n_lat, dst_off=HALO,
                   halos=(xp_ref, xn_ref, m_total))

    def dot_rows(r0, nr):
        a = a_ref[r0:r0 + nr, :]
        sv_ref[r0:r0 + nr, :] = jnp.dot(a, wv_ref[...], preferred_element_type=F32)
        sg_ref[r0:r0 + nr, :] = jnp.dot(a, wg_ref[...], preferred_element_type=F32)

    def epi(r, nr, masks):
        val = _conv_rows(sv_ref, cwv_ref, cbv_ref, HALO + r, nr, masks)
        gat = _conv_rows(sg_ref, cwg_ref, cbg_ref, HALO + r, nr, masks)
        o_ref[r:r + nr, :] = (_silu(gat) * val).astype(o_ref.dtype)

    _interleave(tm, dot_rows, epi, lambda: _boundary_slab_fix(i, tm, n_lat, epi))


def _ffn_up(x, tab, w_up, conv_w, conv_b, layer, *, n_lat):
    m, d = x.shape
    f = w_up.shape[2] // 2
    tm, tn = ROW_TILE, 512
    nj = f // tn
    hb = tm // HALO
    last_hb = m // HALO - 1
    vmap = lambda i, j: (layer, 0, j)
    gmap = lambda i, j: (layer, 0, j + nj)
    return pl.pallas_call(
        functools.partial(_ffn_up_kernel, n_lat=n_lat, m_total=m),
        out_shape=jax.ShapeDtypeStruct((m, f), BF16),
        grid=(m // tm, nj),
        in_specs=[
            pl.BlockSpec((tm, d), lambda i, j: (i, 0)),
            pl.BlockSpec((HALO, d), lambda i, j: (jnp.maximum(i * hb - 1, 0), 0)),
            pl.BlockSpec((HALO, d), lambda i, j: (jnp.minimum((i + 1) * hb, last_hb), 0)),
            pl.BlockSpec((8, d), lambda i, j: (0, 0)),
            pl.BlockSpec((None, d, tn), vmap),
            pl.BlockSpec((None, d, tn), gmap),
            pl.BlockSpec((None, 3, tn), vmap),
            pl.BlockSpec((None, 3, tn), gmap),
            pl.BlockSpec((None, 1, tn), vmap),
            pl.BlockSpec((None, 1, tn), gmap),
        ],
        out_specs=pl.BlockSpec((tm, tn), lambda i, j: (i, j)),
        scratch_shapes=[pltpu.VMEM((tm + 2 * HALO, d), BF16),
                        pltpu.VMEM((tm + 2 * HALO, tn), F32),
                        pltpu.VMEM((tm + 2 * HALO, tn), F32)],
        compiler_params=_cparams(("parallel", "arbitrary")),
        name="convffn_up",
    )(x, x, x, tab, w_up, w_up, conv_w, conv_w, conv_b, conv_b)


def _rope_swap(t):
    lane = lax.broadcasted_iota(jnp.int32, t.shape, 1)
    first = (lane % 64) < 32
    return jnp.where(first, pltpu.roll(t, 96, 1), pltpu.roll(t, 32, 1))


def _qkv_kernel(x_ref, tab_ref, w_ref, gain_ref, cos_ref, sin_ref, o_ref, a_ref, *,
                n_lat, n_q_tiles, q_scale):
    i, j = pl.program_id(0), pl.program_id(1)
    hd = ATTN_HEAD_DIM

    @pl.when(j == 0)
    def _():
        _fill_rows(x_ref, tab_ref, a_ref, i, n_lat=n_lat, dst_off=0)

    tm = a_ref.shape[0]
    tn = w_ref.shape[1]
    chunk = QKV_ROWS
    assert tm % chunk == 0

    def product(r0):
        return jnp.dot(a_ref[r0:r0 + chunk, :], w_ref[...], preferred_element_type=F32)

    @pl.when(j <= n_q_tiles)
    def _():
        is_q = j < n_q_tiles
        gain = jnp.where(is_q, gain_ref[0:1, :], gain_ref[1:2, :])
        post = jnp.where(is_q, q_scale, 1.0)

        def finish(r0, acc):
            cos = cos_ref[r0:r0 + chunk, :]
            sin = sin_ref[r0:r0 + chunk, :]
            for h in range(tn // hd):
                t = acc[:, h * hd:(h + 1) * hd]
                ms = jnp.mean(t * t, axis=-1, keepdims=True)
                t = t * lax.rsqrt(ms + NORM_EPS) * gain
                t = (t * cos + _rope_swap(t) * sin) * post
                o_ref[r0:r0 + chunk, h * hd:(h + 1) * hd] = t.astype(o_ref.dtype)

        acc = product(0)
        for r0 in range(chunk, tm, chunk):
            nxt = product(r0)
            finish(r0 - chunk, acc)
            acc = nxt
        finish(tm - chunk, acc)

    @pl.when(j > n_q_tiles)
    def _():
        o_ref[...] = jnp.dot(a_ref[...], w_ref[...], preferred_element_type=F32).astype(o_ref.dtype)


def _qkv_proj(x, tab, w, gains, cos, sin, layer, *, n_lat, q_dim, q_scale):
    m, d = x.shape
    n = w.shape[2]
    tm, tn = ROW_TILE, 512
    assert (n - q_dim) == 2 * tn
    return pl.pallas_call(
        functools.partial(_qkv_kernel, n_lat=n_lat, n_q_tiles=q_dim // tn, q_scale=q_scale),
        out_shape=jax.ShapeDtypeStruct((m, n), BF16),
        grid=(m // tm, n // tn),
        in_specs=[pl.BlockSpec((tm, d), lambda i, j: (i, 0)),
                  pl.BlockSpec((8, d), lambda i, j: (0, 0)),
                  pl.BlockSpec((None, d, tn), lambda i, j: (layer, 0, j)),
                  pl.BlockSpec((8, ATTN_HEAD_DIM), lambda i, j: (0, 0)),
                  pl.BlockSpec((tm, ATTN_HEAD_DIM), lambda i, j: (i, 0)),
                  pl.BlockSpec((tm, ATTN_HEAD_DIM), lambda i, j: (i, 0))],
        out_specs=pl.BlockSpec((tm, tn), lambda i, j: (i, j)),
        scratch_shapes=[pltpu.VMEM((tm, d), BF16)],
        compiler_params=_cparams(("parallel", "arbitrary")),
        name="attn_qkv_proj",
    )(x, tab, w, gains, cos, sin)


def _mm_res_kernel(a_ref, w_ref, x_ref, g_ref, o_ref, *, n_lat):
    i = pl.program_id(0)
    tm = a_ref.shape[0]
    acc = jnp.dot(a_ref[...], w_ref[...], preferred_element_type=F32)
    rows = i * tm + lax.broadcasted_iota(jnp.int32, (tm, 1), 0)
    gate = jnp.where(rows >= n_lat, g_ref[1:2, :], g_ref[0:1, :])
    o_ref[...] = x_ref[...] + gate * acc


def _matmul_residual(a, w, x, gates, layer, *, n_lat, tn, m_out=None):
    m, k = a.shape
    n = w.shape[2]
    tm = ROW_TILE
    return pl.pallas_call(
        functools.partial(_mm_res_kernel, n_lat=n_lat),
        out_shape=jax.ShapeDtypeStruct((m if m_out is None else m_out, n), F32),
        grid=(m // tm, n // tn),
        in_specs=[pl.BlockSpec((tm, k), lambda i, j: (i, 0)),
                  pl.BlockSpec((None, k, tn), lambda i, j: (layer, 0, j)),
                  pl.BlockSpec((tm, tn), lambda i, j: (i, j)),
                  pl.BlockSpec((8, tn), lambda i, j: (0, j))],
        out_specs=pl.BlockSpec((tm, tn), lambda i, j: (i, j)),
        compiler_params=_cparams(("parallel", "parallel")),
        name="matmul_gated_residual",
    )(a, w, x, gates)


def _ssd_scan_chunk(xs_ref, bc_ref, dt_raw, bias_ref, alog_ref, y_ref, state_ref, col_ref, rowt_ref,
                    *, reverse, hoff, first):
    L = SSD_CHUNK
    P = SSD_HEAD_DIM
    G = SSD_GROUPS
    N = SSD_STATE
    H = xs_ref.shape[1] // P
    E = H // G
    EP = E * P
    PAIR = 2 * P
    assert PAIR == 128 and E % 2 == 0

    @pl.when(first)
    def _():
        state_ref[...] = jnp.zeros_like(state_ref)

    dt = _softplus(dt_raw + bias_ref[...])
    da2 = (-jnp.exp(alog_ref[...]) * dt) * LOG2E
    li = lax.broadcasted_iota(jnp.int32, (L, L), 0)
    si = lax.broadcasted_iota(jnp.int32, (L, L), 1)
    keep = (si >= li) if reverse else (si <= li)
    acs2 = jnp.dot(keep.astype(F32), da2, preferred_element_type=F32,
                   precision=lax.Precision.HIGHEST)
    edge = 0 if reverse else L - 1
    a2t = acs2.T
    r2t = a2t - jnp.log2(dt.T)
    edge_b = jnp.broadcast_to(a2t[:, edge:edge + 1], (128, L))
    rowt_ref[0] = r2t
    rowt_ref[1] = jnp.exp2(edge_b - r2t)
    rowt_ref[2] = jnp.exp2(edge_b)
    for g in range(G):
        lo = hoff + g * E
        col_ref[g, :, 0:E] = acs2[:, lo:lo + E]

    lane_lo = lax.broadcasted_iota(jnp.int32, (L, PAIR), 1) < P
    lane_lo1 = lax.broadcasted_iota(jnp.int32, (1, PAIR), 1) < P
    nt = (((1,), (1,)), ((), ()))

    def group(g, carry):
        b_off = pl.multiple_of(g * N, N)
        c_off = pl.multiple_of((G + g) * N, N)
        bg = bc_ref[:, pl.ds(b_off, N)]
        cg = bc_ref[:, pl.ds(c_off, N)]
        cb = lax.dot_general(cg, bg, nt, preferred_element_type=F32)
        bgt = bg.astype(F32).T
        h_in = state_ref[g]
        y_off = jnp.dot(cg, h_in.astype(BF16), preferred_element_type=F32)
        cols = col_ref[g]
        hrow = pl.multiple_of(hoff + g * E, E)
        rows_r = rowt_ref[0, pl.ds(hrow, E), :]
        rows_f = rowt_ref[1, pl.ds(hrow, E), :]
        rows_c = rowt_ref[2, pl.ds(hrow, E), :]
        x_off = pl.multiple_of(g * EP, EP)
        for pr in range(E // 2):
            e1, e2 = 2 * pr, 2 * pr + 1
            colb1 = jnp.broadcast_to(cols[:, e1:e1 + 1], (L, L))
            colb2 = jnp.broadcast_to(cols[:, e2:e2 + 1], (L, L))
            w1 = (cb * jnp.exp2(jnp.where(keep, colb1 - rows_r[e1:e1 + 1, :], -jnp.inf))).astype(BF16)
            w2 = (cb * jnp.exp2(jnp.where(keep, colb2 - rows_r[e2:e2 + 1, :], -jnp.inf))).astype(BF16)
            s1 = (bgt * rows_f[e1:e1 + 1, :]).astype(BF16)
            s2 = (bgt * rows_f[e2:e2 + 1, :]).astype(BF16)
            lhs = jnp.concatenate([jnp.concatenate([w1, w2], axis=1),
                                   jnp.concatenate([s1, s2], axis=1)], axis=0)
            xp = xs_ref[:, pl.ds(x_off + pr * PAIR, PAIR)]
            zero = jnp.zeros_like(xp)
            xbd = jnp.concatenate([jnp.where(lane_lo, xp, zero), jnp.where(lane_lo, zero, xp)],
                                  axis=0)
            res = jnp.dot(lhs, xbd, preferred_element_type=F32)
            e_in = jnp.exp2(jnp.where(lane_lo, colb1, colb2))
            y_ref[:, pl.ds(x_off + pr * PAIR, PAIR)] = (
                res[0:L] + y_off[:, pr * PAIR:(pr + 1) * PAIR] * e_in)
            cd = jnp.where(lane_lo1, rows_c[e1:e1 + 1, :], rows_c[e2:e2 + 1, :])
            state_ref[g, :, pr * PAIR:(pr + 1) * PAIR] = (
                cd * h_in[:, pr * PAIR:(pr + 1) * PAIR] + res[L:2 * L])
        return carry

    lax.fori_loop(0, G, group, 0, unroll=2)


def _ssd_fwd_kernel(xs_ref, bc_ref, dt_ref, bias_ref, alog_ref, y_ref,
                    state_ref, col_ref, rowt_ref, yacc_ref):
    t = pl.program_id(0)
    _ssd_scan_chunk(xs_ref, bc_ref, dt_ref[...], bias_ref, alog_ref, yacc_ref, state_ref, col_ref,
                    rowt_ref, reverse=False, hoff=0, first=(t == 0))
    y_ref[...] = yacc_ref[...].astype(y_ref.dtype)


def _ssd_bwd_kernel(xs_ref, bc_ref, dt_ref, yf_ref, sz_ref, dskip_ref, nw_ref, bias_ref, alog_ref,
                    g_ref, state_ref, col_ref, rowt_ref, yacc_ref, *, hoff):
    t = pl.program_id(0)
    _ssd_scan_chunk(xs_ref, bc_ref, dt_ref[...], bias_ref, alog_ref, yacc_ref, state_ref, col_ref,
                    rowt_ref, reverse=True, hoff=hoff, first=(t == 0))
    y = (yacc_ref[...] + yf_ref[...].astype(F32) + xs_ref[...].astype(F32) * dskip_ref[...])
    y = y * sz_ref[...].astype(F32)
    ms = jnp.mean(y * y, axis=-1, keepdims=True)
    g_ref[...] = (y * lax.rsqrt(ms + NORM_EPS) * nw_ref[...]).astype(g_ref.dtype)


def _ssd_mixer(zx, dt, bias_fb, alog_fb, dskip, norm_w, *, n_lat):
    m = zx.shape[0]
    di = norm_w.shape[0]
    gn2 = 2 * SSD_GROUPS * SSD_STATE
    L = SSD_CHUNK
    n_chunks = m // L
    n_lat_chunks = n_lat // L
    x_cb = 1
    bc_cb = (2 * di) // gn2
    cf = lambda t: (t + n_lat_chunks) % n_chunks
    scratch = [pltpu.VMEM((SSD_GROUPS, SSD_STATE, di // SSD_GROUPS), F32),
               pltpu.VMEM((SSD_GROUPS, L, 128), F32),
               pltpu.VMEM((3, 128, L), F32),
               pltpu.VMEM((L, di), F32)]
    vec = lambda width: pl.BlockSpec((1, width), lambda t: (0, 0))
    yf = pl.pallas_call(
        _ssd_fwd_kernel,
        out_shape=jax.ShapeDtypeStruct((m, di), BF16),
        grid=(n_chunks,),
        in_specs=[
            pl.BlockSpec((L, di), lambda t: (cf(t), x_cb)),
            pl.BlockSpec((L, gn2), lambda t: (cf(t), bc_cb)),
            pl.BlockSpec((L, 128), lambda t: (cf(t), 0)),
            vec(128), vec(128),
        ],
        out_specs=pl.BlockSpec((L, di), lambda t: (cf(t), 0)),
        scratch_shapes=scratch,
        compiler_params=_cparams(("arbitrary",)),
        name="ssd_scan_fwd",
    )(zx, zx, dt, bias_fb, alog_fb)

    cbk = lambda t: n_chunks - 1 - t
    return pl.pallas_call(
        functools.partial(_ssd_bwd_kernel, hoff=di // SSD_HEAD_DIM),
        out_shape=jax.ShapeDtypeStruct((m, di), BF16),
        grid=(n_chunks,),
        in_specs=[
            pl.BlockSpec((L, di), lambda t: (cbk(t), x_cb)),
            pl.BlockSpec((L, gn2), lambda t: (cbk(t), bc_cb)),
            pl.BlockSpec((L, 128), lambda t: (cbk(t), 0)),
            pl.BlockSpec((L, di), lambda t: (cbk(t), 0)),
            pl.BlockSpec((L, di), lambda t: (cbk(t), 0)),
            vec(di), vec(di), vec(128), vec(128),
        ],
        out_specs=pl.BlockSpec((L, di), lambda t: (cbk(t), 0)),
        scratch_shapes=scratch,
        compiler_params=_cparams(("arbitrary",)),
        name="ssd_scan_bwd_finish",
    )(zx, zx, dt, yf, zx, dskip, norm_w.reshape(1, di), bias_fb, alog_fb)


def _attn_kernel(sink_ref, q_ref, kc_ref, vc_ref, kp_ref, ko_ref, kn_ref, vp_ref, vo_ref, vn_ref,
                 o_ref, *, n_lat_blocks, n_kv):
    qb = pl.program_id(0)
    blk = ATTN_BLOCK
    hd = ATTN_HEAD_DIM
    grp = q_ref.shape[1] // (hd * n_kv)
    rows = grp * blk
    n_ctx = kc_ref.shape[0]
    qi = lax.broadcasted_iota(jnp.int32, (rows, blk), 0) % blk
    kj = lax.broadcasted_iota(jnp.int32, (rows, blk), 1)
    is_lat = qb < n_lat_blocks
    keep_p = (is_lat & (qb > 0)) & (kj >= qi)
    keep_n = (is_lat & (qb < n_lat_blocks - 1)) & (kj <= qi)
    gid = lax.broadcasted_iota(jnp.int32, (rows, 1), 0) // blk
    nt = (((1,), (1,)), ((), ()))
    neg = -jnp.inf
    for kv in range(n_kv):
        ks = slice(kv * hd, (kv + 1) * hd)
        q = jnp.concatenate([q_ref[:, (kv * grp + g) * hd:(kv * grp + g + 1) * hd]
                             for g in range(grp)], axis=0)
        k_all = jnp.concatenate([kc_ref[:, ks], kp_ref[:, ks], ko_ref[:, ks], kn_ref[:, ks]], axis=0)
        v_all = jnp.concatenate([vc_ref[:, ks], vp_ref[:, ks], vo_ref[:, ks], vn_ref[:, ks]], axis=0)
        s = lax.dot_general(q, k_all, nt, preferred_element_type=F32)
        s_c = [s[:, c0:c0 + blk] for c0 in range(0, n_ctx, blk)]
        s_p = jnp.where(keep_p, s[:, n_ctx:n_ctx + blk], neg)
        s_o = jnp.where(is_lat, s[:, n_ctx + blk:n_ctx + 2 * blk], neg)
        s_n = jnp.where(keep_n, s[:, n_ctx + 2 * blk:n_ctx + 3 * blk], neg)
        tiles = s_c + [s_p, s_o, s_n]
        sink = jnp.zeros((rows, 1), F32)
        for g in range(grp):
            sink = jnp.where(gid == g, sink_ref[kv * grp + g] * LOG2E, sink)
        mx_el = tiles[0]
        for t in tiles[1:]:
            mx_el = jnp.maximum(mx_el, t)
        mx = jnp.maximum(jnp.max(mx_el, axis=-1, keepdims=True), sink)
        ps = [jnp.exp2(t - mx) for t in tiles]
        sum_el = ps[0]
        for t in ps[1:]:
            sum_el = sum_el + t
        den = jnp.exp2(sink - mx) + jnp.sum(sum_el, axis=-1, keepdims=True)
        p = jnp.concatenate([t.astype(BF16) for t in ps], axis=1)
        o = jnp.dot(p, v_all, preferred_element_type=F32) / den
        for g in range(grp):
            c0 = (kv * grp + g) * hd
            o_ref[:, c0:c0 + hd] = o[g * blk:(g + 1) * blk, :].astype(o_ref.dtype)


def _attention(qkv, sinks, *, n_lat, n_ctx, n_heads):
    m = qkv.shape[0]
    blk = ATTN_BLOCK
    hd = ATTN_HEAD_DIM
    n_kv = ATTN_KV_HEADS
    q_dim = n_heads * hd
    kv_dim = n_kv * hd
    n_blocks = m // blk
    n_lat_blocks = n_lat // blk
    assert q_dim % kv_dim == 0
    k_col = q_dim // kv_dim
    v_col = k_col + 1
    ctx_row = n_lat // n_ctx
    prv = lambda i: jnp.clip(i - 1, 0, n_lat_blocks - 1)
    own = lambda i: jnp.minimum(i, n_lat_blocks - 1)
    nxt = lambda i: jnp.minimum(i + 1, n_lat_blocks - 1)
    return pl.pallas_call(
        functools.partial(_attn_kernel, n_lat_blocks=n_lat_blocks, n_kv=n_kv),
        out_shape=jax.ShapeDtypeStruct((m, q_dim), BF16),
        grid=(n_blocks,),
        in_specs=[
            pl.BlockSpec(memory_space=pltpu.SMEM),
            pl.BlockSpec((blk, q_dim), lambda i: (i, 0)),
            pl.BlockSpec((n_ctx, kv_dim), lambda i: (ctx_row, k_col)),
            pl.BlockSpec((n_ctx, kv_dim), lambda i: (ctx_row, v_col)),
            pl.BlockSpec((blk, kv_dim), lambda i: (prv(i), k_col)),
            pl.BlockSpec((blk, kv_dim), lambda i: (own(i), k_col)),
            pl.BlockSpec((blk, kv_dim), lambda i: (nxt(i), k_col)),
            pl.BlockSpec((blk, kv_dim), lambda i: (prv(i), v_col)),
            pl.BlockSpec((blk, kv_dim), lambda i: (own(i), v_col)),
            pl.BlockSpec((blk, kv_dim), lambda i: (nxt(i), v_col)),
        ],
        out_specs=pl.BlockSpec((blk, q_dim), lambda i: (i, 0)),
        compiler_params=_cparams(("parallel",)),
        name="window_attention",
    )(sinks, qkv, qkv, qkv, qkv, qkv, qkv, qkv, qkv, qkv)


def _rope_tables(n_lat, n_ctx):
    rows = n_lat // GRID_W
    half = ATTN_HEAD_DIM // 2
    row = jnp.repeat(jnp.arange(rows, dtype=F32), GRID_W)
    col = jnp.tile(jnp.arange(GRID_W, dtype=F32), rows)
    inv_freq = ROPE_THETA ** (-jnp.arange(0, half, 2, dtype=F32) / half)
    ar = row[:, None] * inv_freq[None, :]
    ac = col[:, None] * inv_freq[None, :]
    cos = jnp.concatenate([jnp.cos(ar), jnp.cos(ar), jnp.cos(ac), jnp.cos(ac)], axis=-1)
    sin = jnp.concatenate([-jnp.sin(ar), jnp.sin(ar), -jnp.sin(ac), jnp.sin(ac)], axis=-1)
    cos = jnp.concatenate([cos, jnp.ones((n_ctx, ATTN_HEAD_DIM), F32)], axis=0)
    sin = jnp.concatenate([sin, jnp.zeros((n_ctx, ATTN_HEAD_DIM), F32)], axis=0)
    return cos, sin


def _rows8(*rows):
    d = rows[0].shape[-1]
    pad = [jnp.zeros((d,), F32)] * (8 - len(rows))
    return jnp.stack(list(rows) + pad, axis=0)


def kernel(x, c, ctx, c_ctx, ada_w, ada_b, norm1_w, norm2_w, ssd_w_in, ssd_conv_w, ssd_conv_b,
           ssd_dt_bias_f, ssd_dt_bias_b, ssd_a_log_f, ssd_a_log_b, ssd_d, ssd_norm_w, ssd_w_out,
           attn_w_qkv, attn_q_gain, attn_k_gain, attn_sinks, attn_w_o,
           ffn_w_up, ffn_conv_w, ffn_conv_b, ffn_w_down):
    assert x.shape[0] == 1 and ctx.shape[0] == 1
    n_lat, d = x.shape[1], x.shape[2]
    n_ctx = ctx.shape[1]
    m = n_lat + n_ctx
    depth = ada_w.shape[0]
    assert m % ROW_TILE == 0 and n_lat % n_ctx == 0 and n_lat % SSD_CHUNK == 0
    assert n_ctx % SSD_CHUNK == 0 and n_lat % GRID_W == 0

    xs = jnp.concatenate([x[0], ctx[0]], axis=0)
    c2 = jnp.stack([c[0], c_ctx], axis=1)
    mods = _modulation(c2, ada_w, ada_b).reshape(depth, 2, 6, d)

    di = ssd_norm_w.shape[1]
    n_heads = attn_sinks.shape[1]
    q_dim = n_heads * ATTN_HEAD_DIM
    cos, sin = _rope_tables(n_lat, n_ctx)

    w_in, w_out = ssd_w_in.astype(BF16), ssd_w_out.astype(BF16)
    w_qkv, w_o = attn_w_qkv.astype(BF16), attn_w_o.astype(BF16)
    w_up, w_down = ffn_w_up.astype(BF16), ffn_w_down.astype(BF16)
    ssd_cb = ssd_conv_b[:, None, :]
    ffn_cb = ffn_conv_b[:, None, :]

    for i in range(depth):
        last = i == depth - 1
        j = i // 2
        sh1, sc1, g1, sh2, sc2, g2 = [mods[i, :, q, :] for q in range(6)]
        tab1 = _rows8(norm1_w[i], sc1[0], sh1[0], sc1[1], sh1[1])
        tab2 = _rows8(norm2_w[i], sc2[0], sh2[0], sc2[1], sh2[1])
        gate1 = _rows8(g1[0], g1[1])
        gate2 = _rows8(g2[0], g2[1])
        if i % 2 == 0:
            zx, dt = _ssd_in_proj(xs, tab1, w_in, ssd_conv_w, ssd_cb, j, n_lat=n_lat, d_inner=di)
            bias_fb = jnp.concatenate([ssd_dt_bias_f[j], ssd_dt_bias_b[j]]).reshape(1, -1)
            alog_fb = jnp.concatenate([ssd_a_log_f[j], ssd_a_log_b[j]]).reshape(1, -1)
            dskip = jnp.repeat(ssd_d[j], SSD_HEAD_DIM).reshape(1, di)
            gated = _ssd_mixer(zx, dt, bias_fb, alog_fb, dskip, ssd_norm_w[j], n_lat=n_lat)
            xs = _matmul_residual(gated, w_out, xs, gate1, j, n_lat=n_lat, tn=512)
        else:
            gains = _rows8(attn_q_gain[j], attn_k_gain[j])
            qkv = _qkv_proj(xs, tab1, w_qkv, gains, cos, sin, j, n_lat=n_lat,
                            q_dim=q_dim, q_scale=ATTN_HEAD_DIM ** -0.5 * LOG2E)
            o = _attention(qkv, attn_sinks[j], n_lat=n_lat, n_ctx=n_ctx, n_heads=n_heads)
            xs = _matmul_residual(o, w_o, xs, gate1, j, n_lat=n_lat, tn=512)
        a = _ffn_up(xs, tab2, w_up, ffn_conv_w, ffn_cb, i, n_lat=n_lat)
        xs = _matmul_residual(a, w_down, xs, gate2, i, n_lat=n_lat, tn=512,
                              m_out=(n_lat if last else None))
    return xs[None]
```

```python
import functools

import jax
import jax.numpy as jnp
from jax import lax
from jax.experimental import pallas as pl
from jax.experimental.pallas import tpu as pltpu

F32 = jnp.float32
BF16 = jnp.bfloat16

NORM_EPS = 1e-6
GRID_W = 64
ROPE_THETA = 10000.0
LOG2E = 1.4426950408889634

SSD_HEAD_DIM = 64
SSD_GROUPS = 8
SSD_STATE = 128
SSD_CHUNK = 128
ATTN_HEAD_DIM = 128
ATTN_KV_HEADS = 4
ATTN_BLOCK = 128

ROW_TILE = 768
HALO = 16
QKV_ROWS = 192
DOT_CHUNKS = (160,) * 5
EPI_SLABS = (128,) * 6
VMEM_LIMIT = 56 * 1024 * 1024


def _cparams(sem):
    return pltpu.CompilerParams(dimension_semantics=sem, vmem_limit_bytes=VMEM_LIMIT)


def _silu(v):
    return v / (1.0 + jnp.exp(-v))


def _softplus(v):
    return jnp.maximum(v, 0.0) + jnp.log(1.0 + jnp.exp(-jnp.abs(v)))


def _modulation_kernel(c_ref, w_ref, b_ref, o_ref, *, kc):
    d = c_ref.shape[0]
    tn = w_ref.shape[2]

    def body(k, acc):
        a0, a1 = acc
        k0 = pl.multiple_of(k * kc, kc)
        sv = _silu(c_ref[pl.ds(k0, kc), :])
        w = w_ref[0, pl.ds(k0, kc), :]
        a0 = a0 + jnp.sum(w * sv[:, 0:1], axis=0, keepdims=True)
        a1 = a1 + jnp.sum(w * sv[:, 1:2], axis=0, keepdims=True)
        return a0, a1

    z = jnp.zeros((1, tn), F32)
    a0, a1 = lax.fori_loop(0, d // kc, body, (z, z))
    bias = b_ref[0]
    o_ref[0, 0:1, :] = a0 + bias
    o_ref[0, 1:2, :] = a1 + bias


def _modulation(c2, ada_w, ada_b):
    depth, d, n = ada_w.shape
    tn = 1024
    return pl.pallas_call(
        functools.partial(_modulation_kernel, kc=256),
        out_shape=jax.ShapeDtypeStruct((depth, 2, n), F32),
        grid=(depth, n // tn),
        in_specs=[
            pl.BlockSpec((d, 2), lambda l, j: (0, 0)),
            pl.BlockSpec((1, d, tn), lambda l, j: (l, 0, j)),
            pl.BlockSpec((1, 1, tn), lambda l, j: (l, 0, j)),
        ],
        out_specs=pl.BlockSpec((1, 2, tn), lambda l, j: (l, 0, j)),
        compiler_params=_cparams(("parallel", "parallel")),
        name="adaln_modulation",
    )(c2, ada_w, ada_b.reshape(depth, 1, n))


def _mod_vectors(tab_ref):
    w = tab_ref[0:1, :]
    return (w * (1.0 + tab_ref[1:2, :]), tab_ref[2:3, :],
            w * (1.0 + tab_ref[3:4, :]), tab_ref[4:5, :])


def _norm_mod_store(x, vecs, is_ctx, a_ref, dst0):
    mult = jnp.where(is_ctx, vecs[2], vecs[0])
    shift = jnp.where(is_ctx, vecs[3], vecs[1])
    ms = jnp.mean(x * x, axis=-1, keepdims=True)
    a_ref[pl.ds(dst0, x.shape[0]), :] = (x * lax.rsqrt(ms + NORM_EPS) * mult + shift).astype(BF16)


def _fill_rows(x_ref, tab_ref, a_ref, i, *, n_lat, dst_off, halos=None):
    tm = x_ref.shape[0]
    sub = 128
    vecs = _mod_vectors(tab_ref)

    def body(r, carry):
        r0 = pl.multiple_of(r * sub, sub)
        _norm_mod_store(x_ref[pl.ds(r0, sub), :], vecs, (i * tm + r0) >= n_lat, a_ref,
                        pl.multiple_of(dst_off + r0, HALO))
        return carry

    lax.fori_loop(0, tm // sub, body, 0)
    if halos is not None:
        xp_ref, xn_ref, m_total = halos
        _norm_mod_store(xp_ref[...], vecs, (i * tm - HALO) >= n_lat, a_ref, 0)
        _norm_mod_store(xn_ref[...], vecs, ((i + 1) * tm) >= n_lat, a_ref, dst_off + tm)
        start = i * tm

        @pl.when((start == 0) | (start == n_lat))
        def _():
            a_ref[0:HALO, :] = jnp.zeros((HALO, a_ref.shape[1]), BF16)

        @pl.when((start + tm == n_lat) | (start + tm == m_total))
        def _():
            a_ref[dst_off + tm:dst_off + tm + HALO, :] = jnp.zeros((HALO, a_ref.shape[1]), BF16)


def _conv_rows(stage_ref, cw_ref, cb_ref, r0, nrows, masks=None):
    ext = stage_ref[pl.ds(r0 - 8, nrows + 16), :]
    up = pltpu.roll(ext, 1, 0)[8:8 + nrows]
    v = ext[8:8 + nrows]
    dn = pltpu.roll(ext, nrows + 15, 0)[8:8 + nrows]
    if masks is not None:
        up = jnp.where(masks[0], up, 0.0)
        dn = jnp.where(masks[1], dn, 0.0)
    return cw_ref[0:1, :] * up + cw_ref[1:2, :] * v + cw_ref[2:3, :] * dn + cb_ref[...]


def _interleave(tm, dot_rows, epi_rows, slab_fix):
    assert sum(DOT_CHUNKS) == tm + 2 * HALO and sum(EPI_SLABS) == tm
    slabs = []
    r = 0
    for n in EPI_SLABS:
        slabs.append((r, n))
        r += n
    done = 0
    r0 = 0
    for nr in DOT_CHUNKS:
        dot_rows(r0, nr)
        while done < len(slabs) and HALO + sum(slabs[done]) + 1 <= r0:
            epi_rows(*slabs[done], None)
            done += 1
        r0 += nr
    for slab in slabs[done:]:
        epi_rows(*slab, None)
    slab_fix()


def _boundary_slab_fix(i, tm, n_lat, epi_rows):
    rb = n_lat % tm
    if rb:
        @pl.when(i == n_lat // tm)
        def _():
            rid = lax.broadcasted_iota(jnp.int32, (2 * HALO, 1), 0) + (rb - HALO)
            epi_rows(rb - HALO, 2 * HALO, (rid != rb, rid != rb - 1))


def _ssd_in_kernel(x_ref, xp_ref, xn_ref, tab_ref, w_ref, wdt_ref, cw_ref, cb_ref,
                   o_ref, dt_ref, a_ref, stage_ref, *, n_lat, m_total, n_z_tiles):
    i, j = pl.program_id(0), pl.program_id(1)
    tm = x_ref.shape[0]

    @pl.when(j == 0)
    def _():
        _fill_rows(x_ref, tab_ref, a_ref, i, n_lat=n_lat, dst_off=HALO,
                   halos=(xp_ref, xn_ref, m_total))
        dt_ref[...] = jnp.dot(a_ref[HALO:HALO + tm, :], wdt_ref[...], preferred_element_type=F32)

    def dot_rows(r0, nr):
        stage_ref[r0:r0 + nr, :] = jnp.dot(a_ref[r0:r0 + nr, :], w_ref[...],
                                           preferred_element_type=F32)

    @pl.when(j < n_z_tiles)
    def _():
        def epi(r, nr, masks):
            o_ref[r:r + nr, :] = _silu(stage_ref[HALO + r:HALO + r + nr, :]).astype(o_ref.dtype)

        _interleave(tm, dot_rows, epi, lambda: None)

    @pl.when(j >= n_z_tiles)
    def _():
        def epi(r, nr, masks):
            cv = _conv_rows(stage_ref, cw_ref, cb_ref, HALO + r, nr, masks)
            o_ref[r:r + nr, :] = _silu(cv).astype(o_ref.dtype)

        _interleave(tm, dot_rows, epi, lambda: _boundary_slab_fix(i, tm, n_lat, epi))


def _ssd_in_proj(x, tab, w_in, conv_w, conv_b, layer, *, n_lat, d_inner):
    m, d = x.shape
    tm, tn = ROW_TILE, 1024
    n_main = 2 * d_inner + 2 * SSD_GROUPS * SSD_STATE
    ndt = w_in.shape[2] - n_main
    n_z_tiles = d_inner // tn
    hb = tm // HALO
    last_hb = m // HALO - 1
    cmap = lambda i, j: (layer, 0, jnp.maximum(j - n_z_tiles, 0))
    return pl.pallas_call(
        functools.partial(_ssd_in_kernel, n_lat=n_lat, m_total=m, n_z_tiles=n_z_tiles),
        out_shape=(jax.ShapeDtypeStruct((m, n_main), BF16), jax.ShapeDtypeStruct((m, ndt), F32)),
        grid=(m // tm, n_main // tn),
        in_specs=[
            pl.BlockSpec((tm, d), lambda i, j: (i, 0)),
            pl.BlockSpec((HALO, d), lambda i, j: (jnp.maximum(i * hb - 1, 0), 0)),
            pl.BlockSpec((HALO, d), lambda i, j: (jnp.minimum((i + 1) * hb, last_hb), 0)),
            pl.BlockSpec((8, d), lambda i, j: (0, 0)),
            pl.BlockSpec((None, d, tn), lambda i, j: (layer, 0, j)),
            pl.BlockSpec((None, d, ndt), lambda i, j: (layer, 0, n_main // ndt)),
            pl.BlockSpec((None, 3, tn), cmap),
            pl.BlockSpec((None, 1, tn), cmap),
        ],
        out_specs=(pl.BlockSpec((tm, tn), lambda i, j: (i, j)),
                   pl.BlockSpec((tm, ndt), lambda i, j: (i, 0))),
        scratch_shapes=[pltpu.VMEM((tm + 2 * HALO, d), BF16),
                        pltpu.VMEM((tm + 2 * HALO, tn), F32)],
        compiler_params=_cparams(("parallel", "arbitrary")),
        name="ssd_in_proj",
    )(x, x, x, tab, w_in, w_in, conv_w, conv_b)


def _ffn_up_kernel(x_ref, xp_ref, xn_ref, tab_ref, wv_ref, wg_ref, cwv_ref, cwg_ref, cbv_ref,
                   cbg_ref, o_ref, a_ref, sv_ref, sg_ref, *, n_lat, m_total):
    i, j = pl.program_id(0), pl.program_id(1)
    tm = x_ref.shape[0]

    @pl.when(j == 0)
    def _():
        _fill_rows(x_ref, tab_ref, a_ref, i, n_lat=n_lat, dst_off=HALO,
                   halos=(xp_ref, xn_ref, m_total))

    def dot_rows(r0, nr):
        a = a_ref[r0:r0 + nr, :]
        sv_ref[r0:r0 + nr, :] = jnp.dot(a, wv_ref[...], preferred_element_type=F32)
        sg_ref[r0:r0 + nr, :] = jnp.dot(a, wg_ref[...], preferred_element_type=F32)

    def epi(r, nr, masks):
        val = _conv_rows(sv_ref, cwv_ref, cbv_ref, HALO + r, nr, masks)
        gat = _conv_rows(sg_ref, cwg_ref, cbg_ref, HALO + r, nr, masks)
        o_ref[r:r + nr, :] = (_silu(gat) * val).astype(o_ref.dtype)

    _interleave(tm, dot_rows, epi, lambda: _boundary_slab_fix(i, tm, n_lat, epi))


def _ffn_up(x, tab, w_up, conv_w, conv_b, layer, *, n_lat):
    m, d = x.shape
    f = w_up.shape[2] // 2
    tm, tn = ROW_TILE, 512
    nj = f // tn
    hb = tm // HALO
    last_hb = m // HALO - 1
    vmap = lambda i, j: (layer, 0, j)
    gmap = lambda i, j: (layer, 0, j + nj)
    return pl.pallas_call(
        functools.partial(_ffn_up_kernel, n_lat=n_lat, m_total=m),
        out_shape=jax.ShapeDtypeStruct((m, f), BF16),
        grid=(m // tm, nj),
        in_specs=[
            pl.BlockSpec((tm, d), lambda i, j: (i, 0)),
            pl.BlockSpec((HALO, d), lambda i, j: (jnp.maximum(i * hb - 1, 0), 0)),
            pl.BlockSpec((HALO, d), lambda i, j: (jnp.minimum((i + 1) * hb, last_hb), 0)),
            pl.BlockSpec((8, d), lambda i, j: (0, 0)),
            pl.BlockSpec((None, d, tn), vmap),
            pl.BlockSpec((None, d, tn), gmap),
            pl.BlockSpec((None, 3, tn), vmap),
            pl.BlockSpec((None, 3, tn), gmap),
            pl.BlockSpec((None, 1, tn), vmap),
            pl.BlockSpec((None, 1, tn), gmap),
        ],
        out_specs=pl.BlockSpec((tm, tn), lambda i, j: (i, j)),
        scratch_shapes=[pltpu.VMEM((tm + 2 * HALO, d), BF16),
                        pltpu.VMEM((tm + 2 * HALO, tn), F32),
                        pltpu.VMEM((tm + 2 * HALO, tn), F32)],
        compiler_params=_cparams(("parallel", "arbitrary")),
        name="convffn_up",
    )(x, x, x, tab, w_up, w_up, conv_w, conv_w, conv_b, conv_b)


def _rope_swap(t):
    lane = lax.broadcasted_iota(jnp.int32, t.shape, 1)
    first = (lane % 64) < 32
    return jnp.where(first, pltpu.roll(t, 96, 1), pltpu.roll(t, 32, 1))


def _qkv_kernel(x_ref, tab_ref, w_ref, gain_ref, cos_ref, sin_ref, o_ref, a_ref, *,
                n_lat, n_q_tiles, q_scale):
    i, j = pl.program_id(0), pl.program_id(1)
    hd = ATTN_HEAD_DIM

    @pl.when(j == 0)
    def _():
        _fill_rows(x_ref, tab_ref, a_ref, i, n_lat=n_lat, dst_off=0)

    tm = a_ref.shape[0]
    tn = w_ref.shape[1]
    chunk = QKV_ROWS
    assert tm % chunk == 0

    def product(r0):
        return jnp.dot(a_ref[r0:r0 + chunk, :], w_ref[...], preferred_element_type=F32)

    @pl.when(j <= n_q_tiles)
    def _():
        is_q = j < n_q_tiles
        gain = jnp.where(is_q, gain_ref[0:1, :], gain_ref[1:2, :])
        post = jnp.where(is_q, q_scale, 1.0)

        def finish(r0, acc):
            cos = cos_ref[r0:r0 + chunk, :]
            sin = sin_ref[r0:r0 + chunk, :]
            for h in range(tn // hd):
                t = acc[:, h * hd:(h + 1) * hd]
                ms = jnp.mean(t * t, axis=-1, keepdims=True)
                t = t * lax.rsqrt(ms + NORM_EPS) * gain
                t = (t * cos + _rope_swap(t) * sin) * post
                o_ref[r0:r0 + chunk, h * hd:(h + 1) * hd] = t.astype(o_ref.dtype)

        acc = product(0)
        for r0 in range(chunk, tm, chunk):
            nxt = product(r0)
            finish(r0 - chunk, acc)
            acc = nxt
        finish(tm - chunk, acc)

    @pl.when(j > n_q_tiles)
    def _():
        o_ref[...] = jnp.dot(a_ref[...], w_ref[...], preferred_element_type=F32).astype(o_ref.dtype)


def _qkv_proj(x, tab, w, gains, cos, sin, layer, *, n_lat, q_dim, q_scale):
    m, d = x.shape
    n = w.shape[2]
    tm, tn = ROW_TILE, 512
    assert (n - q_dim) == 2 * tn
    return pl.pallas_call(
        functools.partial(_qkv_kernel, n_lat=n_lat, n_q_tiles=q_dim // tn, q_scale=q_scale),
        out_shape=jax.ShapeDtypeStruct((m, n), BF16),
        grid=(m // tm, n // tn),
        in_specs=[pl.BlockSpec((tm, d), lambda i, j: (i, 0)),
                  pl.BlockSpec((8, d), lambda i, j: (0, 0)),
                  pl.BlockSpec((None, d, tn), lambda i, j: (layer, 0, j)),
                  pl.BlockSpec((8, ATTN_HEAD_DIM), lambda i, j: (0, 0)),
                  pl.BlockSpec((tm, ATTN_HEAD_DIM), lambda i, j: (i, 0)),
                  pl.BlockSpec((tm, ATTN_HEAD_DIM), lambda i, j: (i, 0))],
        out_specs=pl.BlockSpec((tm, tn), lambda i, j: (i, j)),
        scratch_shapes=[pltpu.VMEM((tm, d), BF16)],
        compiler_params=_cparams(("parallel", "arbitrary")),
        name="attn_qkv_proj",
    )(x, tab, w, gains, cos, sin)


def _mm_res_kernel(a_ref, w_ref, x_ref, g_ref, o_ref, *, n_lat):
    i = pl.program_id(0)
    tm = a_ref.shape[0]
    acc = jnp.dot(a_ref[...], w_ref[...], preferred_element_type=F32)
    rows = i * tm + lax.broadcasted_iota(jnp.int32, (tm, 1), 0)
    gate = jnp.where(rows >= n_lat, g_ref[1:2, :], g_ref[0:1, :])
    o_ref[...] = x_ref[...] + gate * acc


def _matmul_residual(a, w, x, gates, layer, *, n_lat, tn, m_out=None):
    m, k = a.shape
    n = w.shape[2]
    tm = ROW_TILE
    return pl.pallas_call(
        functools.partial(_mm_res_kernel, n_lat=n_lat),
        out_shape=jax.ShapeDtypeStruct((m if m_out is None else m_out, n), F32),
        grid=(m // tm, n // tn),
        in_specs=[pl.BlockSpec((tm, k), lambda i, j: (i, 0)),
                  pl.BlockSpec((None, k, tn), lambda i, j: (layer, 0, j)),
                  pl.BlockSpec((tm, tn), lambda i, j: (i, j)),
                  pl.BlockSpec((8, tn), lambda i, j: (0, j))],
        out_specs=pl.BlockSpec((tm, tn), lambda i, j: (i, j)),
        compiler_params=_cparams(("parallel", "parallel")),
        name="matmul_gated_residual",
    )(a, w, x, gates)


def _ssd_scan_chunk(xs_ref, bc_ref, dt_raw, bias_ref, alog_ref, y_ref, state_ref, col_ref, rowt_ref,
                    *, reverse, hoff, first):
    L = SSD_CHUNK
    P = SSD_HEAD_DIM
    G = SSD_GROUPS
    N = SSD_STATE
    H = xs_ref.shape[1] // P
    E = H // G
    EP = E * P
    PAIR = 2 * P
    assert PAIR == 128 and E % 2 == 0

    @pl.when(first)
    def _():
        state_ref[...] = jnp.zeros_like(state_ref)

    dt = _softplus(dt_raw + bias_ref[...])
    da2 = (-jnp.exp(alog_ref[...]) * dt) * LOG2E
    li = lax.broadcasted_iota(jnp.int32, (L, L), 0)
    si = lax.broadcasted_iota(jnp.int32, (L, L), 1)
    keep = (si >= li) if reverse else (si <= li)
    acs2 = jnp.dot(keep.astype(F32), da2, preferred_element_type=F32,
                   precision=lax.Precision.HIGHEST)
    edge = 0 if reverse else L - 1
    a2t = acs2.T
    r2t = a2t - jnp.log2(dt.T)
    edge_b = jnp.broadcast_to(a2t[:, edge:edge + 1], (128, L))
    rowt_ref[0] = r2t
    rowt_ref[1] = jnp.exp2(edge_b - r2t)
    rowt_ref[2] = jnp.exp2(edge_b)
    for g in range(G):
        lo = hoff + g * E
        col_ref[g, :, 0:E] = acs2[:, lo:lo + E]

    lane_lo = lax.broadcasted_iota(jnp.int32, (L, PAIR), 1) < P
    lane_lo1 = lax.broadcasted_iota(jnp.int32, (1, PAIR), 1) < P
    nt = (((1,), (1,)), ((), ()))

    def group(g, carry):
        b_off = pl.multiple_of(g * N, N)
        c_off = pl.multiple_of((G + g) * N, N)
        bg = bc_ref[:, pl.ds(b_off, N)]
        cg = bc_ref[:, pl.ds(c_off, N)]
        cb = lax.dot_general(cg, bg, nt, preferred_element_type=F32)
        bgt = bg.astype(F32).T
        h_in = state_ref[g]
        y_off = jnp.dot(cg, h_in.astype(BF16), preferred_element_type=F32)
        cols = col_ref[g]
        hrow = pl.multiple_of(hoff + g * E, E)
        rows_r = rowt_ref[0, pl.ds(hrow, E), :]
        rows_f = rowt_ref[1, pl.ds(hrow, E), :]
        rows_c = rowt_ref[2, pl.ds(hrow, E), :]
        x_off = pl.multiple_of(g * EP, EP)
        for pr in range(E // 2):
            e1, e2 = 2 * pr, 2 * pr + 1
            colb1 = jnp.broadcast_to(cols[:, e1:e1 + 1], (L, L))
            colb2 = jnp.broadcast_to(cols[:, e2:e2 + 1], (L, L))
            w1 = (cb * jnp.exp2(jnp.where(keep, colb1 - rows_r[e1:e1 + 1, :], -jnp.inf))).astype(BF16)
            w2 = (cb * jnp.exp2(jnp.where(keep, colb2 - rows_r[e2:e2 + 1, :], -jnp.inf))).astype(BF16)
            s1 = (bgt * rows_f[e1:e1 + 1, :]).astype(BF16)
            s2 = (bgt * rows_f[e2:e2 + 1, :]).astype(BF16)
            lhs = jnp.concatenate([jnp.concatenate([w1, w2], axis=1),
                                   jnp.concatenate([s1, s2], axis=1)], axis=0)
            xp = xs_ref[:, pl.ds(x_off + pr * PAIR, PAIR)]
            zero = jnp.zeros_like(xp)
            xbd = jnp.concatenate([jnp.where(lane_lo, xp, zero), jnp.where(lane_lo, zero, xp)],
                                  axis=0)
            res = jnp.dot(lhs, xbd, preferred_element_type=F32)
            e_in = jnp.exp2(jnp.where(lane_lo, colb1, colb2))
            y_ref[:, pl.ds(x_off + pr * PAIR, PAIR)] = (
                res[0:L] + y_off[:, pr * PAIR:(pr + 1) * PAIR] * e_in)
            cd = jnp.where(lane_lo1, rows_c[e1:e1 + 1, :], rows_c[e2:e2 + 1, :])
            state_ref[g, :, pr * PAIR:(pr + 1) * PAIR] = (
                cd * h_in[:, pr * PAIR:(pr + 1) * PAIR] + res[L:2 * L])
        return carry

    lax.fori_loop(0, G, group, 0, unroll=4)


def _ssd_fwd_kernel(xs_ref, bc_ref, dt_ref, bias_ref, alog_ref, y_ref,
                    state_ref, col_ref, rowt_ref, yacc_ref):
    t = pl.program_id(0)
    _ssd_scan_chunk(xs_ref, bc_ref, dt_ref[...], bias_ref, alog_ref, yacc_ref, state_ref, col_ref,
                    rowt_ref, reverse=False, hoff=0, first=(t == 0))
    y_ref[...] = yacc_ref[...].astype(y_ref.dtype)


def _ssd_bwd_kernel(xs_ref, bc_ref, dt_ref, yf_ref, sz_ref, dskip_ref, nw_ref, bias_ref, alog_ref,
                    g_ref, state_ref, col_ref, rowt_ref, yacc_ref, *, hoff):
    t = pl.program_id(0)
    _ssd_scan_chunk(xs_ref, bc_ref, dt_ref[...], bias_ref, alog_ref, yacc_ref, state_ref, col_ref,
                    rowt_ref, reverse=True, hoff=hoff, first=(t == 0))
    y = (yacc_ref[...] + yf_ref[...].astype(F32) + xs_ref[...].astype(F32) * dskip_ref[...])
    y = y * sz_ref[...].astype(F32)
    ms = jnp.mean(y * y, axis=-1, keepdims=True)
    g_ref[...] = (y * lax.rsqrt(ms + NORM_EPS) * nw_ref[...]).astype(g_ref.dtype)


def _ssd_mixer(zx, dt, bias_fb, alog_fb, dskip, norm_w, *, n_lat):
    m = zx.shape[0]
    di = norm_w.shape[0]
    gn2 = 2 * SSD_GROUPS * SSD_STATE
    L = SSD_CHUNK
    n_chunks = m // L
    n_lat_chunks = n_lat // L
    x_cb = 1
    bc_cb = (2 * di) // gn2
    cf = lambda t: (t + n_lat_chunks) % n_chunks
    scratch = [pltpu.VMEM((SSD_GROUPS, SSD_STATE, di // SSD_GROUPS), F32),
               pltpu.VMEM((SSD_GROUPS, L, 128), F32),
               pltpu.VMEM((3, 128, L), F32),
               pltpu.VMEM((L, di), F32)]
    vec = lambda width: pl.BlockSpec((1, width), lambda t: (0, 0))
    yf = pl.pallas_call(
        _ssd_fwd_kernel,
        out_shape=jax.ShapeDtypeStruct((m, di), BF16),
        grid=(n_chunks,),
        in_specs=[
            pl.BlockSpec((L, di), lambda t: (cf(t), x_cb)),
            pl.BlockSpec((L, gn2), lambda t: (cf(t), bc_cb)),
            pl.BlockSpec((L, 128), lambda t: (cf(t), 0)),
            vec(128), vec(128),
        ],
        out_specs=pl.BlockSpec((L, di), lambda t: (cf(t), 0)),
        scratch_shapes=scratch,
        compiler_params=_cparams(("arbitrary",)),
        name="ssd_scan_fwd",
    )(zx, zx, dt, bias_fb, alog_fb)

    cbk = lambda t: n_chunks - 1 - t
    return pl.pallas_call(
        functools.partial(_ssd_bwd_kernel, hoff=di // SSD_HEAD_DIM),
        out_shape=jax.ShapeDtypeStruct((m, di), BF16),
        grid=(n_chunks,),
        in_specs=[
            pl.BlockSpec((L, di), lambda t: (cbk(t), x_cb)),
            pl.BlockSpec((L, gn2), lambda t: (cbk(t), bc_cb)),
            pl.BlockSpec((L, 128), lambda t: (cbk(t), 0)),
            pl.BlockSpec((L, di), lambda t: (cbk(t), 0)),
            pl.BlockSpec((L, di), lambda t: (cbk(t), 0)),
            vec(di), vec(di), vec(128), vec(128),
        ],
        out_specs=pl.BlockSpec((L, di), lambda t: (cbk(t), 0)),
        scratch_shapes=scratch,
        compiler_params=_cparams(("arbitrary",)),
        name="ssd_scan_bwd_finish",
    )(zx, zx, dt, yf, zx, dskip, norm_w.reshape(1, di), bias_fb, alog_fb)


def _attn_kernel(sink_ref, q_ref, kc_ref, vc_ref, kp_ref, ko_ref, kn_ref, vp_ref, vo_ref, vn_ref,
                 o_ref, *, n_lat_blocks, n_kv):
    qb = pl.program_id(0)
    blk = ATTN_BLOCK
    hd = ATTN_HEAD_DIM
    grp = q_ref.shape[1] // (hd * n_kv)
    rows = grp * blk
    n_ctx = kc_ref.shape[0]
    qi = lax.broadcasted_iota(jnp.int32, (rows, blk), 0) % blk
    kj = lax.broadcasted_iota(jnp.int32, (rows, blk), 1)
    is_lat = qb < n_lat_blocks
    keep_p = (is_lat & (qb > 0)) & (kj >= qi)
    keep_n = (is_lat & (qb < n_lat_blocks - 1)) & (kj <= qi)
    gid = lax.broadcasted_iota(jnp.int32, (rows, 1), 0) // blk
    nt = (((1,), (1,)), ((), ()))
    neg = -jnp.inf
    for kv in range(n_kv):
        ks = slice(kv * hd, (kv + 1) * hd)
        q = jnp.concatenate([q_ref[:, (kv * grp + g) * hd:(kv * grp + g + 1) * hd]
                             for g in range(grp)], axis=0)
        k_all = jnp.concatenate([kc_ref[:, ks], kp_ref[:, ks], ko_ref[:, ks], kn_ref[:, ks]], axis=0)
        v_all = jnp.concatenate([vc_ref[:, ks], vp_ref[:, ks], vo_ref[:, ks], vn_ref[:, ks]], axis=0)
        s = lax.dot_general(q, k_all, nt, preferred_element_type=F32)
        s_c = [s[:, c0:c0 + blk] for c0 in range(0, n_ctx, blk)]
        s_p = jnp.where(keep_p, s[:, n_ctx:n_ctx + blk], neg)
        s_o = jnp.where(is_lat, s[:, n_ctx + blk:n_ctx + 2 * blk], neg)
        s_n = jnp.where(keep_n, s[:, n_ctx + 2 * blk:n_ctx + 3 * blk], neg)
        tiles = s_c + [s_p, s_o, s_n]
        sink = jnp.zeros((rows, 1), F32)
        for g in range(grp):
            sink = jnp.where(gid == g, sink_ref[kv * grp + g] * LOG2E, sink)
        mx_el = tiles[0]
        for t in tiles[1:]:
            mx_el = jnp.maximum(mx_el, t)
        mx = jnp.maximum(jnp.max(mx_el, axis=-1, keepdims=True), sink)
        mx_b = jnp.broadcast_to(mx, (rows, blk))
        ps = [jnp.exp2(t - mx_b) for t in tiles]
        sum_el = ps[0]
        for t in ps[1:]:
            sum_el = sum_el + t
        den = jnp.exp2(sink - mx) + jnp.sum(sum_el, axis=-1, keepdims=True)
        p = jnp.concatenate([t.astype(BF16) for t in ps], axis=1)
        o = jnp.dot(p, v_all, preferred_element_type=F32) / den
        for g in range(grp):
            c0 = (kv * grp + g) * hd
            o_ref[:, c0:c0 + hd] = o[g * blk:(g + 1) * blk, :].astype(o_ref.dtype)


def _attention(qkv, sinks, *, n_lat, n_ctx, n_heads):
    m = qkv.shape[0]
    blk = ATTN_BLOCK
    hd = ATTN_HEAD_DIM
    n_kv = ATTN_KV_HEADS
    q_dim = n_heads * hd
    kv_dim = n_kv * hd
    n_blocks = m // blk
    n_lat_blocks = n_lat // blk
    assert q_dim % kv_dim == 0
    k_col = q_dim // kv_dim
    v_col = k_col + 1
    ctx_row = n_lat // n_ctx
    prv = lambda i: jnp.clip(i - 1, 0, n_lat_blocks - 1)
    own = lambda i: jnp.minimum(i, n_lat_blocks - 1)
    nxt = lambda i: jnp.minimum(i + 1, n_lat_blocks - 1)
    return pl.pallas_call(
        functools.partial(_attn_kernel, n_lat_blocks=n_lat_blocks, n_kv=n_kv),
        out_shape=jax.ShapeDtypeStruct((m, q_dim), BF16),
        grid=(n_blocks,),
        in_specs=[
            pl.BlockSpec(memory_space=pltpu.SMEM),
            pl.BlockSpec((blk, q_dim), lambda i: (i, 0)),
            pl.BlockSpec((n_ctx, kv_dim), lambda i: (ctx_row, k_col)),
            pl.BlockSpec((n_ctx, kv_dim), lambda i: (ctx_row, v_col)),
            pl.BlockSpec((blk, kv_dim), lambda i: (prv(i), k_col)),
            pl.BlockSpec((blk, kv_dim), lambda i: (own(i), k_col)),
            pl.BlockSpec((blk, kv_dim), lambda i: (nxt(i), k_col)),
            pl.BlockSpec((blk, kv_dim), lambda i: (prv(i), v_col)),
            pl.BlockSpec((blk, kv_dim), lambda i: (own(i), v_col)),
            pl.BlockSpec((blk, kv_dim), lambda i: (nxt(i), v_col)),
        ],
        out_specs=pl.BlockSpec((blk, q_dim), lambda i: (i, 0)),
        compiler_params=_cparams(("parallel",)),
        name="window_attention",
    )(sinks, qkv, qkv, qkv, qkv, qkv, qkv, qkv, qkv, qkv)


def _rope_tables(n_lat, n_ctx):
    rows = n_lat // GRID_W
    half = ATTN_HEAD_DIM // 2
    row = jnp.repeat(jnp.arange(rows, dtype=F32), GRID_W)
    col = jnp.tile(jnp.arange(GRID_W, dtype=F32), rows)
    inv_freq = ROPE_THETA ** (-jnp.arange(0, half, 2, dtype=F32) / half)
    ar = row[:, None] * inv_freq[None, :]
    ac = col[:, None] * inv_freq[None, :]
    cos = jnp.concatenate([jnp.cos(ar), jnp.cos(ar), jnp.cos(ac), jnp.cos(ac)], axis=-1)
    sin = jnp.concatenate([-jnp.sin(ar), jnp.sin(ar), -jnp.sin(ac), jnp.sin(ac)], axis=-1)
    cos = jnp.concatenate([cos, jnp.ones((n_ctx, ATTN_HEAD_DIM), F32)], axis=0)
    sin = jnp.concatenate([sin, jnp.zeros((n_ctx, ATTN_HEAD_DIM), F32)], axis=0)
    return cos, sin


def _rows8(*rows):
    d = rows[0].shape[-1]
    pad = [jnp.zeros((d,), F32)] * (8 - len(rows))
    return jnp.stack(list(rows) + pad, axis=0)


def kernel(x, c, ctx, c_ctx, ada_w, ada_b, norm1_w, norm2_w, ssd_w_in, ssd_conv_w, ssd_conv_b,
           ssd_dt_bias_f, ssd_dt_bias_b, ssd_a_log_f, ssd_a_log_b, ssd_d, ssd_norm_w, ssd_w_out,
           attn_w_qkv, attn_q_gain, attn_k_gain, attn_sinks, attn_w_o,
           ffn_w_up, ffn_conv_w, ffn_conv_b, ffn_w_down):
    assert x.shape[0] == 1 and ctx.shape[0] == 1
    n_lat, d = x.shape[1], x.shape[2]
    n_ctx = ctx.shape[1]
    m = n_lat + n_ctx
    depth = ada_w.shape[0]
    assert m % ROW_TILE == 0 and n_lat % n_ctx == 0 and n_lat % SSD_CHUNK == 0
    assert n_ctx % SSD_CHUNK == 0 and n_lat % GRID_W == 0

    xs = jnp.concatenate([x[0], ctx[0]], axis=0)
    c2 = jnp.stack([c[0], c_ctx], axis=1)
    mods = _modulation(c2, ada_w, ada_b).reshape(depth, 2, 6, d)

    di = ssd_norm_w.shape[1]
    n_heads = attn_sinks.shape[1]
    q_dim = n_heads * ATTN_HEAD_DIM
    cos, sin = _rope_tables(n_lat, n_ctx)

    w_in, w_out = ssd_w_in.astype(BF16), ssd_w_out.astype(BF16)
    w_qkv, w_o = attn_w_qkv.astype(BF16), attn_w_o.astype(BF16)
    w_up, w_down = ffn_w_up.astype(BF16), ffn_w_down.astype(BF16)
    ssd_cb = ssd_conv_b[:, None, :]
    ffn_cb = ffn_conv_b[:, None, :]

    for i in range(depth):
        last = i == depth - 1
        j = i // 2
        sh1, sc1, g1, sh2, sc2, g2 = [mods[i, :, q, :] for q in range(6)]
        tab1 = _rows8(norm1_w[i], sc1[0], sh1[0], sc1[1], sh1[1])
        tab2 = _rows8(norm2_w[i], sc2[0], sh2[0], sc2[1], sh2[1])
        gate1 = _rows8(g1[0], g1[1])
        gate2 = _rows8(g2[0], g2[1])
        if i % 2 == 0:
            zx, dt = _ssd_in_proj(xs, tab1, w_in, ssd_conv_w, ssd_cb, j, n_lat=n_lat, d_inner=di)
            bias_fb = jnp.concatenate([ssd_dt_bias_f[j], ssd_dt_bias_b[j]]).reshape(1, -1)
            alog_fb = jnp.concatenate([ssd_a_log_f[j], ssd_a_log_b[j]]).reshape(1, -1)
            dskip = jnp.repeat(ssd_d[j], SSD_HEAD_DIM).reshape(1, di)
            gated = _ssd_mixer(zx, dt, bias_fb, alog_fb, dskip, ssd_norm_w[j], n_lat=n_lat)
            xs = _matmul_residual(gated, w_out, xs, gate1, j, n_lat=n_lat, tn=512)
        else:
            gains = _rows8(attn_q_gain[j], attn_k_gain[j])
            qkv = _qkv_proj(xs, tab1, w_qkv, gains, cos, sin, j, n_lat=n_lat,
                            q_dim=q_dim, q_scale=ATTN_HEAD_DIM ** -0.5 * LOG2E)
            o = _attention(qkv, attn_sinks[j], n_lat=n_lat, n_ctx=n_ctx, n_heads=n_heads)
            xs = _matmul_residual(o, w_o, xs, gate1, j, n_lat=n_lat, tn=512)
        a = _ffn_up(xs, tab2, w_up, ffn_conv_w, ffn_cb, i, n_lat=n_lat)
        xs = _matmul_residual(a, w_down, xs, gate2, i, n_lat=n_lat, tn=512,
                              m_out=(n_lat if last else None))
    return xs[None]
```

```python
import functools

import jax
import jax.numpy as jnp
from jax import lax
from jax.experimental import pallas as pl
from jax.experimental.pallas import tpu as pltpu

F32 = jnp.float32
BF16 = jnp.bfloat16

NORM_EPS = 1e-6
GRID_W = 64
ROPE_THETA = 10000.0
LOG2E = 1.4426950408889634

SSD_HEAD_DIM = 64
SSD_GROUPS = 8
SSD_STATE = 128
SSD_CHUNK = 128
ATTN_HEAD_DIM = 128
ATTN_KV_HEADS = 4
ATTN_BLOCK = 128

ROW_TILE = 768
CONV_TILE = 1408
HALO = 16
QKV_ROWS = 192
DOT_CHUNKS = (288,) * 5
EPI_SLABS = (128,) * 11
VMEM_LIMIT = 56 * 1024 * 1024


def _cparams(sem):
    return pltpu.CompilerParams(dimension_semantics=sem, vmem_limit_bytes=VMEM_LIMIT)


def _silu(v):
    return v / (1.0 + jnp.exp(-v))


def _softplus(v):
    return jnp.maximum(v, 0.0) + jnp.log(1.0 + jnp.exp(-jnp.abs(v)))


def _modulation_kernel(c_ref, w_ref, b_ref, o_ref, *, kc):
    d = c_ref.shape[0]
    tn = w_ref.shape[2]

    def body(k, acc):
        a0, a1 = acc
        k0 = pl.multiple_of(k * kc, kc)
        sv = _silu(c_ref[pl.ds(k0, kc), :])
        w = w_ref[0, pl.ds(k0, kc), :]
        a0 = a0 + jnp.sum(w * sv[:, 0:1], axis=0, keepdims=True)
        a1 = a1 + jnp.sum(w * sv[:, 1:2], axis=0, keepdims=True)
        return a0, a1

    z = jnp.zeros((1, tn), F32)
    a0, a1 = lax.fori_loop(0, d // kc, body, (z, z))
    bias = b_ref[0]
    o_ref[0, 0:1, :] = a0 + bias
    o_ref[0, 1:2, :] = a1 + bias


def _modulation(c2, ada_w, ada_b):
    depth, d, n = ada_w.shape
    tn = 1024
    return pl.pallas_call(
        functools.partial(_modulation_kernel, kc=256),
        out_shape=jax.ShapeDtypeStruct((depth, 2, n), F32),
        grid=(depth, n // tn),
        in_specs=[
            pl.BlockSpec((d, 2), lambda l, j: (0, 0)),
            pl.BlockSpec((1, d, tn), lambda l, j: (l, 0, j)),
            pl.BlockSpec((1, 1, tn), lambda l, j: (l, 0, j)),
        ],
        out_specs=pl.BlockSpec((1, 2, tn), lambda l, j: (l, 0, j)),
        compiler_params=_cparams(("parallel", "parallel")),
        name="adaln_modulation",
    )(c2, ada_w, ada_b.reshape(depth, 1, n))


def _mod_vectors(tab_ref):
    w = tab_ref[0:1, :]
    return (w * (1.0 + tab_ref[1:2, :]), tab_ref[2:3, :],
            w * (1.0 + tab_ref[3:4, :]), tab_ref[4:5, :])


def _norm_mod_store(x, vecs, is_ctx, a_ref, dst0):
    mult = jnp.where(is_ctx, vecs[2], vecs[0])
    shift = jnp.where(is_ctx, vecs[3], vecs[1])
    ms = jnp.mean(x * x, axis=-1, keepdims=True)
    a_ref[pl.ds(dst0, x.shape[0]), :] = (x * lax.rsqrt(ms + NORM_EPS) * mult + shift).astype(BF16)


def _fill_rows(x_ref, tab_ref, a_ref, i, *, n_lat, dst_off, halos=None):
    tm = x_ref.shape[0]
    sub = 128
    vecs = _mod_vectors(tab_ref)

    def body(r, carry):
        r0 = pl.multiple_of(r * sub, sub)
        _norm_mod_store(x_ref[pl.ds(r0, sub), :], vecs, (i * tm + r0) >= n_lat, a_ref,
                        pl.multiple_of(dst_off + r0, HALO))
        return carry

    lax.fori_loop(0, tm // sub, body, 0)
    if halos is not None:
        xp_ref, xn_ref, m_total = halos
        _norm_mod_store(xp_ref[...], vecs, (i * tm - HALO) >= n_lat, a_ref, 0)
        _norm_mod_store(xn_ref[...], vecs, ((i + 1) * tm) >= n_lat, a_ref, dst_off + tm)
        start = i * tm

        @pl.when((start == 0) | (start == n_lat))
        def _():
            a_ref[0:HALO, :] = jnp.zeros((HALO, a_ref.shape[1]), BF16)

        @pl.when((start + tm == n_lat) | (start + tm == m_total))
        def _():
            a_ref[dst_off + tm:dst_off + tm + HALO, :] = jnp.zeros((HALO, a_ref.shape[1]), BF16)


def _conv_rows(stage_ref, cw_ref, cb_ref, r0, nrows, masks=None):
    ext = stage_ref[pl.ds(r0 - 8, nrows + 16), :]
    up = pltpu.roll(ext, 1, 0)[8:8 + nrows]
    v = ext[8:8 + nrows]
    dn = pltpu.roll(ext, nrows + 15, 0)[8:8 + nrows]
    if masks is not None:
        up = jnp.where(masks[0], up, 0.0)
        dn = jnp.where(masks[1], dn, 0.0)
    return cw_ref[0:1, :] * up + cw_ref[1:2, :] * v + cw_ref[2:3, :] * dn + cb_ref[...]


def _interleave(tm, dot_rows, epi_rows, slab_fix):
    assert sum(DOT_CHUNKS) == tm + 2 * HALO and sum(EPI_SLABS) == tm
    slabs = []
    r = 0
    for n in EPI_SLABS:
        slabs.append((r, n))
        r += n
    done = 0
    r0 = 0
    for nr in DOT_CHUNKS:
        dot_rows(r0, nr)
        while done < len(slabs) and HALO + sum(slabs[done]) + 1 <= r0:
            epi_rows(*slabs[done], None)
            done += 1
        r0 += nr
    for slab in slabs[done:]:
        epi_rows(*slab, None)
    slab_fix()


def _boundary_slab_fix(i, tm, n_lat, epi_rows):
    rb = n_lat % tm
    if rb:
        @pl.when(i == n_lat // tm)
        def _():
            rid = lax.broadcasted_iota(jnp.int32, (2 * HALO, 1), 0) + (rb - HALO)
            epi_rows(rb - HALO, 2 * HALO, (rid != rb, rid != rb - 1))


def _ssd_in_kernel(x_ref, xp_ref, xn_ref, tab_ref, w_ref, wdt_ref, cw_ref, cb_ref,
                   o_ref, dt_ref, a_ref, stage_ref, *, n_lat, m_total, n_z_tiles):
    i, j = pl.program_id(0), pl.program_id(1)
    tm = x_ref.shape[0]

    @pl.when(j == 0)
    def _():
        _fill_rows(x_ref, tab_ref, a_ref, i, n_lat=n_lat, dst_off=HALO,
                   halos=(xp_ref, xn_ref, m_total))
        dt_ref[...] = jnp.dot(a_ref[HALO:HALO + tm, :], wdt_ref[...], preferred_element_type=F32)

    def dot_rows(r0, nr):
        stage_ref[r0:r0 + nr, :] = jnp.dot(a_ref[r0:r0 + nr, :], w_ref[...],
                                           preferred_element_type=F32)

    @pl.when(j < n_z_tiles)
    def _():
        def epi(r, nr, masks):
            o_ref[r:r + nr, :] = _silu(stage_ref[HALO + r:HALO + r + nr, :]).astype(o_ref.dtype)

        _interleave(tm, dot_rows, epi, lambda: None)

    @pl.when(j >= n_z_tiles)
    def _():
        def epi(r, nr, masks):
            cv = _conv_rows(stage_ref, cw_ref, cb_ref, HALO + r, nr, masks)
            o_ref[r:r + nr, :] = _silu(cv).astype(o_ref.dtype)

        _interleave(tm, dot_rows, epi, lambda: _boundary_slab_fix(i, tm, n_lat, epi))


def _ssd_in_proj(x, tab, w_in, conv_w, conv_b, layer, *, n_lat, d_inner):
    m, d = x.shape
    tm, tn = CONV_TILE, 1024
    n_main = 2 * d_inner + 2 * SSD_GROUPS * SSD_STATE
    ndt = w_in.shape[2] - n_main
    n_z_tiles = d_inner // tn
    hb = tm // HALO
    last_hb = m // HALO - 1
    cmap = lambda i, j: (layer, 0, jnp.maximum(j - n_z_tiles, 0))
    return pl.pallas_call(
        functools.partial(_ssd_in_kernel, n_lat=n_lat, m_total=m, n_z_tiles=n_z_tiles),
        out_shape=(jax.ShapeDtypeStruct((m, n_main), BF16), jax.ShapeDtypeStruct((m, ndt), F32)),
        grid=(m // tm, n_main // tn),
        in_specs=[
            pl.BlockSpec((tm, d), lambda i, j: (i, 0)),
            pl.BlockSpec((HALO, d), lambda i, j: (jnp.maximum(i * hb - 1, 0), 0)),
            pl.BlockSpec((HALO, d), lambda i, j: (jnp.minimum((i + 1) * hb, last_hb), 0)),
            pl.BlockSpec((8, d), lambda i, j: (0, 0)),
            pl.BlockSpec((None, d, tn), lambda i, j: (layer, 0, j)),
            pl.BlockSpec((None, d, ndt), lambda i, j: (layer, 0, n_main // ndt)),
            pl.BlockSpec((None, 3, tn), cmap),
            pl.BlockSpec((None, 1, tn), cmap),
        ],
        out_specs=(pl.BlockSpec((tm, tn), lambda i, j: (i, j)),
                   pl.BlockSpec((tm, ndt), lambda i, j: (i, 0))),
        scratch_shapes=[pltpu.VMEM((tm + 2 * HALO, d), BF16),
                        pltpu.VMEM((tm + 2 * HALO, tn), F32)],
        compiler_params=_cparams(("parallel", "arbitrary")),
        name="ssd_in_proj",
    )(x, x, x, tab, w_in, w_in, conv_w, conv_b)


def _ffn_up_kernel(x_ref, xp_ref, xn_ref, tab_ref, wv_ref, wg_ref, cwv_ref, cwg_ref, cbv_ref,
                   cbg_ref, o_ref, a_ref, sv_ref, sg_ref, *, n_lat, m_total):
    i, j = pl.program_id(0), pl.program_id(1)
    tm = x_ref.shape[0]

    @pl.when(j == 0)
    def _():
        _fill_rows(x_ref, tab_ref, a_ref, i, n_lat=n_lat, dst_off=HALO,
                   halos=(xp_ref, xn_ref, m_total))

    def dot_rows(r0, nr):
        a = a_ref[r0:r0 + nr, :]
        sv_ref[r0:r0 + nr, :] = jnp.dot(a, wv_ref[...], preferred_element_type=F32)
        sg_ref[r0:r0 + nr, :] = jnp.dot(a, wg_ref[...], preferred_element_type=F32)

    def epi(r, nr, masks):
        val = _conv_rows(sv_ref, cwv_ref, cbv_ref, HALO + r, nr, masks)
        gat = _conv_rows(sg_ref, cwg_ref, cbg_ref, HALO + r, nr, masks)
        o_ref[r:r + nr, :] = (_silu(gat) * val).astype(o_ref.dtype)

    _interleave(tm, dot_rows, epi, lambda: _boundary_slab_fix(i, tm, n_lat, epi))


def _ffn_up(x, tab, w_up, conv_w, conv_b, layer, *, n_lat):
    m, d = x.shape
    f = w_up.shape[2] // 2
    tm, tn = CONV_TILE, 512
    nj = f // tn
    hb = tm // HALO
    last_hb = m // HALO - 1
    vmap = lambda i, j: (layer, 0, j)
    gmap = lambda i, j: (layer, 0, j + nj)
    return pl.pallas_call(
        functools.partial(_ffn_up_kernel, n_lat=n_lat, m_total=m),
        out_shape=jax.ShapeDtypeStruct((m, f), BF16),
        grid=(m // tm, nj),
        in_specs=[
            pl.BlockSpec((tm, d), lambda i, j: (i, 0)),
            pl.BlockSpec((HALO, d), lambda i, j: (jnp.maximum(i * hb - 1, 0), 0)),
            pl.BlockSpec((HALO, d), lambda i, j: (jnp.minimum((i + 1) * hb, last_hb), 0)),
            pl.BlockSpec((8, d), lambda i, j: (0, 0)),
            pl.BlockSpec((None, d, tn), vmap),
            pl.BlockSpec((None, d, tn), gmap),
            pl.BlockSpec((None, 3, tn), vmap),
            pl.BlockSpec((None, 3, tn), gmap),
            pl.BlockSpec((None, 1, tn), vmap),
            pl.BlockSpec((None, 1, tn), gmap),
        ],
        out_specs=pl.BlockSpec((tm, tn), lambda i, j: (i, j)),
        scratch_shapes=[pltpu.VMEM((tm + 2 * HALO, d), BF16),
                        pltpu.VMEM((tm + 2 * HALO, tn), F32),
                        pltpu.VMEM((tm + 2 * HALO, tn), F32)],
        compiler_params=_cparams(("parallel", "arbitrary")),
        name="convffn_up",
    )(x, x, x, tab, w_up, w_up, conv_w, conv_w, conv_b, conv_b)


def _rope_swap(t):
    lane = lax.broadcasted_iota(jnp.int32, t.shape, 1)
    first = (lane % 64) < 32
    return jnp.where(first, pltpu.roll(t, 96, 1), pltpu.roll(t, 32, 1))


def _qkv_kernel(x_ref, tab_ref, w_ref, gain_ref, cos_ref, sin_ref, o_ref, a_ref, *,
                n_lat, n_q_tiles, q_scale):
    i, j = pl.program_id(0), pl.program_id(1)
    hd = ATTN_HEAD_DIM

    @pl.when(j == 0)
    def _():
        _fill_rows(x_ref, tab_ref, a_ref, i, n_lat=n_lat, dst_off=0)

    tm = a_ref.shape[0]
    tn = w_ref.shape[1]
    chunk = QKV_ROWS
    assert tm % chunk == 0

    def product(r0):
        return jnp.dot(a_ref[r0:r0 + chunk, :], w_ref[...], preferred_element_type=F32)

    @pl.when(j <= n_q_tiles)
    def _():
        is_q = j < n_q_tiles
        gain = jnp.where(is_q, gain_ref[0:1, :], gain_ref[1:2, :])
        post = jnp.where(is_q, q_scale, 1.0)

        def finish(r0, acc):
            cos = cos_ref[r0:r0 + chunk, :]
            sin = sin_ref[r0:r0 + chunk, :]
            for h in range(tn // hd):
                t = acc[:, h * hd:(h + 1) * hd]
                ms = jnp.mean(t * t, axis=-1, keepdims=True)
                t = t * lax.rsqrt(ms + NORM_EPS) * gain
                t = (t * cos + _rope_swap(t) * sin) * post
                o_ref[r0:r0 + chunk, h * hd:(h + 1) * hd] = t.astype(o_ref.dtype)

        acc = product(0)
        for r0 in range(chunk, tm, chunk):
            nxt = product(r0)
            finish(r0 - chunk, acc)
            acc = nxt
        finish(tm - chunk, acc)

    @pl.when(j > n_q_tiles)
    def _():
        o_ref[...] = jnp.dot(a_ref[...], w_ref[...], preferred_element_type=F32).astype(o_ref.dtype)


def _qkv_proj(x, tab, w, gains, cos, sin, layer, *, n_lat, q_dim, q_scale):
    m, d = x.shape
    n = w.shape[2]
    tm, tn = ROW_TILE, 512
    assert (n - q_dim) == 2 * tn
    return pl.pallas_call(
        functools.partial(_qkv_kernel, n_lat=n_lat, n_q_tiles=q_dim // tn, q_scale=q_scale),
        out_shape=jax.ShapeDtypeStruct((m, n), BF16),
        grid=(m // tm, n // tn),
        in_specs=[pl.BlockSpec((tm, d), lambda i, j: (i, 0)),
                  pl.BlockSpec((8, d), lambda i, j: (0, 0)),
                  pl.BlockSpec((None, d, tn), lambda i, j: (layer, 0, j)),
                  pl.BlockSpec((8, ATTN_HEAD_DIM), lambda i, j: (0, 0)),
                  pl.BlockSpec((tm, ATTN_HEAD_DIM), lambda i, j: (i, 0)),
                  pl.BlockSpec((tm, ATTN_HEAD_DIM), lambda i, j: (i, 0))],
        out_specs=pl.BlockSpec((tm, tn), lambda i, j: (i, j)),
        scratch_shapes=[pltpu.VMEM((tm, d), BF16)],
        compiler_params=_cparams(("parallel", "arbitrary")),
        name="attn_qkv_proj",
    )(x, tab, w, gains, cos, sin)


def _mm_res_kernel(a_ref, w_ref, x_ref, g_ref, o_ref, *, n_lat):
    i = pl.program_id(0)
    tm = a_ref.shape[0]
    acc = jnp.dot(a_ref[...], w_ref[...], preferred_element_type=F32)
    rows = i * tm + lax.broadcasted_iota(jnp.int32, (tm, 1), 0)
    gate = jnp.where(rows >= n_lat, g_ref[1:2, :], g_ref[0:1, :])
    o_ref[...] = x_ref[...] + gate * acc


def _matmul_residual(a, w, x, gates, layer, *, n_lat, tn, m_out=None):
    m, k = a.shape
    n = w.shape[2]
    tm = ROW_TILE
    return pl.pallas_call(
        functools.partial(_mm_res_kernel, n_lat=n_lat),
        out_shape=jax.ShapeDtypeStruct((m if m_out is None else m_out, n), F32),
        grid=(m // tm, n // tn),
        in_specs=[pl.BlockSpec((tm, k), lambda i, j: (i, 0)),
                  pl.BlockSpec((None, k, tn), lambda i, j: (layer, 0, j)),
                  pl.BlockSpec((tm, tn), lambda i, j: (i, j)),
                  pl.BlockSpec((8, tn), lambda i, j: (0, j))],
        out_specs=pl.BlockSpec((tm, tn), lambda i, j: (i, j)),
        compiler_params=_cparams(("parallel", "parallel")),
        name="matmul_gated_residual",
    )(a, w, x, gates)


def _ssd_scan_chunk(xs_ref, bc_ref, dt_raw, bias_ref, alog_ref, y_ref, state_ref, col_ref, rowt_ref,
                    *, reverse, hoff, first):
    L = SSD_CHUNK
    P = SSD_HEAD_DIM
    G = SSD_GROUPS
    N = SSD_STATE
    H = xs_ref.shape[1] // P
    E = H // G
    EP = E * P
    PAIR = 2 * P
    assert PAIR == 128 and E % 2 == 0

    @pl.when(first)
    def _():
        state_ref[...] = jnp.zeros_like(state_ref)

    dt = _softplus(dt_raw + bias_ref[...])
    da2 = (-jnp.exp(alog_ref[...]) * dt) * LOG2E
    li = lax.broadcasted_iota(jnp.int32, (L, L), 0)
    si = lax.broadcasted_iota(jnp.int32, (L, L), 1)
    keep = (si >= li) if reverse else (si <= li)
    acs2 = jnp.dot(keep.astype(F32), da2, preferred_element_type=F32,
                   precision=lax.Precision.HIGHEST)
    edge = 0 if reverse else L - 1
    a2t = acs2.T
    r2t = a2t - jnp.log2(dt.T)
    edge_b = jnp.broadcast_to(a2t[:, edge:edge + 1], (128, L))
    rowt_ref[0] = r2t
    rowt_ref[1] = jnp.exp2(edge_b - r2t)
    rowt_ref[2] = jnp.exp2(edge_b)
    for g in range(G):
        lo = hoff + g * E
        col_ref[g, :, 0:E] = acs2[:, lo:lo + E]

    lane_lo = lax.broadcasted_iota(jnp.int32, (L, PAIR), 1) < P
    lane_lo1 = lax.broadcasted_iota(jnp.int32, (1, PAIR), 1) < P
    nt = (((1,), (1,)), ((), ()))

    def group(g, carry):
        b_off = pl.multiple_of(g * N, N)
        c_off = pl.multiple_of((G + g) * N, N)
        bg = bc_ref[:, pl.ds(b_off, N)]
        cg = bc_ref[:, pl.ds(c_off, N)]
        cb = lax.dot_general(cg, bg, nt, preferred_element_type=F32)
        bgt = bg.astype(F32).T
        h_in = state_ref[g]
        y_off = jnp.dot(cg, h_in.astype(BF16), preferred_element_type=F32)
        cols = col_ref[g]
        hrow = pl.multiple_of(hoff + g * E, E)
        rows_r = rowt_ref[0, pl.ds(hrow, E), :]
        rows_f = rowt_ref[1, pl.ds(hrow, E), :]
        rows_c = rowt_ref[2, pl.ds(hrow, E), :]
        x_off = pl.multiple_of(g * EP, EP)
        for pr in range(E // 2):
            e1, e2 = 2 * pr, 2 * pr + 1
            colb1 = jnp.broadcast_to(cols[:, e1:e1 + 1], (L, L))
            colb2 = jnp.broadcast_to(cols[:, e2:e2 + 1], (L, L))
            w1 = (cb * jnp.exp2(jnp.where(keep, colb1 - rows_r[e1:e1 + 1, :], -jnp.inf))).astype(BF16)
            w2 = (cb * jnp.exp2(jnp.where(keep, colb2 - rows_r[e2:e2 + 1, :], -jnp.inf))).astype(BF16)
            s1 = (bgt * rows_f[e1:e1 + 1, :]).astype(BF16)
            s2 = (bgt * rows_f[e2:e2 + 1, :]).astype(BF16)
            lhs = jnp.concatenate([jnp.concatenate([w1, w2], axis=1),
                                   jnp.concatenate([s1, s2], axis=1)], axis=0)
            xp = xs_ref[:, pl.ds(x_off + pr * PAIR, PAIR)]
            zero = jnp.zeros_like(xp)
            xbd = jnp.concatenate([jnp.where(lane_lo, xp, zero), jnp.where(lane_lo, zero, xp)],
                                  axis=0)
            res = jnp.dot(lhs, xbd, preferred_element_type=F32)
            e_in = jnp.exp2(jnp.where(lane_lo, colb1, colb2))
            y_ref[:, pl.ds(x_off + pr * PAIR, PAIR)] = (
                res[0:L] + y_off[:, pr * PAIR:(pr + 1) * PAIR] * e_in)
            cd = jnp.where(lane_lo1, rows_c[e1:e1 + 1, :], rows_c[e2:e2 + 1, :])
            state_ref[g, :, pr * PAIR:(pr + 1) * PAIR] = (
                cd * h_in[:, pr * PAIR:(pr + 1) * PAIR] + res[L:2 * L])
        return carry

    lax.fori_loop(0, G, group, 0, unroll=4)


def _ssd_fwd_kernel(xs_ref, bc_ref, dt_ref, bias_ref, alog_ref, y_ref,
                    state_ref, col_ref, rowt_ref, yacc_ref):
    t = pl.program_id(0)
    _ssd_scan_chunk(xs_ref, bc_ref, dt_ref[...], bias_ref, alog_ref, yacc_ref, state_ref, col_ref,
                    rowt_ref, reverse=False, hoff=0, first=(t == 0))
    y_ref[...] = yacc_ref[...].astype(y_ref.dtype)


def _ssd_bwd_kernel(xs_ref, bc_ref, dt_ref, yf_ref, sz_ref, dskip_ref, nw_ref, bias_ref, alog_ref,
                    g_ref, state_ref, col_ref, rowt_ref, yacc_ref, *, hoff):
    t = pl.program_id(0)
    _ssd_scan_chunk(xs_ref, bc_ref, dt_ref[...], bias_ref, alog_ref, yacc_ref, state_ref, col_ref,
                    rowt_ref, reverse=True, hoff=hoff, first=(t == 0))
    y = (yacc_ref[...] + yf_ref[...].astype(F32) + xs_ref[...].astype(F32) * dskip_ref[...])
    y = y * sz_ref[...].astype(F32)
    ms = jnp.mean(y * y, axis=-1, keepdims=True)
    g_ref[...] = (y * lax.rsqrt(ms + NORM_EPS) * nw_ref[...]).astype(g_ref.dtype)


def _ssd_mixer(zx, dt, bias_fb, alog_fb, dskip, norm_w, *, n_lat):
    m = zx.shape[0]
    di = norm_w.shape[0]
    gn2 = 2 * SSD_GROUPS * SSD_STATE
    L = SSD_CHUNK
    n_chunks = m // L
    n_lat_chunks = n_lat // L
    x_cb = 1
    bc_cb = (2 * di) // gn2
    cf = lambda t: (t + n_lat_chunks) % n_chunks
    scratch = [pltpu.VMEM((SSD_GROUPS, SSD_STATE, di // SSD_GROUPS), F32),
               pltpu.VMEM((SSD_GROUPS, L, 128), F32),
               pltpu.VMEM((3, 128, L), F32),
               pltpu.VMEM((L, di), F32)]
    vec = lambda width: pl.BlockSpec((1, width), lambda t: (0, 0))
    yf = pl.pallas_call(
        _ssd_fwd_kernel,
        out_shape=jax.ShapeDtypeStruct((m, di), BF16),
        grid=(n_chunks,),
        in_specs=[
            pl.BlockSpec((L, di), lambda t: (cf(t), x_cb)),
            pl.BlockSpec((L, gn2), lambda t: (cf(t), bc_cb)),
            pl.BlockSpec((L, 128), lambda t: (cf(t), 0)),
            vec(128), vec(128),
        ],
        out_specs=pl.BlockSpec((L, di), lambda t: (cf(t), 0)),
        scratch_shapes=scratch,
        compiler_params=_cparams(("arbitrary",)),
        name="ssd_scan_fwd",
    )(zx, zx, dt, bias_fb, alog_fb)

    cbk = lambda t: n_chunks - 1 - t
    return pl.pallas_call(
        functools.partial(_ssd_bwd_kernel, hoff=di // SSD_HEAD_DIM),
        out_shape=jax.ShapeDtypeStruct((m, di), BF16),
        grid=(n_chunks,),
        in_specs=[
            pl.BlockSpec((L, di), lambda t: (cbk(t), x_cb)),
            pl.BlockSpec((L, gn2), lambda t: (cbk(t), bc_cb)),
            pl.BlockSpec((L, 128), lambda t: (cbk(t), 0)),
            pl.BlockSpec((L, di), lambda t: (cbk(t), 0)),
            pl.BlockSpec((L, di), lambda t: (cbk(t), 0)),
            vec(di), vec(di), vec(128), vec(128),
        ],
        out_specs=pl.BlockSpec((L, di), lambda t: (cbk(t), 0)),
        scratch_shapes=scratch,
        compiler_params=_cparams(("arbitrary",)),
        name="ssd_scan_bwd_finish",
    )(zx, zx, dt, yf, zx, dskip, norm_w.reshape(1, di), bias_fb, alog_fb)


def _attn_kernel(sink_ref, q_ref, kc_ref, vc_ref, kp_ref, ko_ref, kn_ref, vp_ref, vo_ref, vn_ref,
                 o_ref, *, n_lat_blocks, n_kv):
    qb = pl.program_id(0)
    blk = ATTN_BLOCK
    hd = ATTN_HEAD_DIM
    grp = q_ref.shape[1] // (hd * n_kv)
    rows = grp * blk
    n_ctx = kc_ref.shape[0]
    qi = lax.broadcasted_iota(jnp.int32, (rows, blk), 0) % blk
    kj = lax.broadcasted_iota(jnp.int32, (rows, blk), 1)
    is_lat = qb < n_lat_blocks
    keep_p = (is_lat & (qb > 0)) & (kj >= qi)
    keep_n = (is_lat & (qb < n_lat_blocks - 1)) & (kj <= qi)
    gid = lax.broadcasted_iota(jnp.int32, (rows, 1), 0) // blk
    nt = (((1,), (1,)), ((), ()))
    neg = -jnp.inf
    for kv in range(n_kv):
        ks = slice(kv * hd, (kv + 1) * hd)
        q = jnp.concatenate([q_ref[:, (kv * grp + g) * hd:(kv * grp + g + 1) * hd]
                             for g in range(grp)], axis=0)
        k_all = jnp.concatenate([kc_ref[:, ks], kp_ref[:, ks], ko_ref[:, ks], kn_ref[:, ks]], axis=0)
        v_all = jnp.concatenate([vc_ref[:, ks], vp_ref[:, ks], vo_ref[:, ks], vn_ref[:, ks]], axis=0)
        s = lax.dot_general(q, k_all, nt, preferred_element_type=F32)
        s_c = [s[:, c0:c0 + blk] for c0 in range(0, n_ctx, blk)]
        s_p = jnp.where(keep_p, s[:, n_ctx:n_ctx + blk], neg)
        s_o = jnp.where(is_lat, s[:, n_ctx + blk:n_ctx + 2 * blk], neg)
        s_n = jnp.where(keep_n, s[:, n_ctx + 2 * blk:n_ctx + 3 * blk], neg)
        tiles = s_c + [s_p, s_o, s_n]
        sink = jnp.zeros((rows, 1), F32)
        for g in range(grp):
            sink = jnp.where(gid == g, sink_ref[kv * grp + g] * LOG2E, sink)
        mx_el = tiles[0]
        for t in tiles[1:]:
            mx_el = jnp.maximum(mx_el, t)
        mx = jnp.maximum(jnp.max(mx_el, axis=-1, keepdims=True), sink)
        mx_b = jnp.broadcast_to(mx, (rows, blk))
        ps = [jnp.exp2(t - mx_b) for t in tiles]
        sum_el = ps[0]
        for t in ps[1:]:
            sum_el = sum_el + t
        den = jnp.exp2(sink - mx) + jnp.sum(sum_el, axis=-1, keepdims=True)
        p = jnp.concatenate([t.astype(BF16) for t in ps], axis=1)
        o = jnp.dot(p, v_all, preferred_element_type=F32) / den
        for g in range(grp):
            c0 = (kv * grp + g) * hd
            o_ref[:, c0:c0 + hd] = o[g * blk:(g + 1) * blk, :].astype(o_ref.dtype)


def _attention(qkv, sinks, *, n_lat, n_ctx, n_heads):
    m = qkv.shape[0]
    blk = ATTN_BLOCK
    hd = ATTN_HEAD_DIM
    n_kv = ATTN_KV_HEADS
    q_dim = n_heads * hd
    kv_dim = n_kv * hd
    n_blocks = m // blk
    n_lat_blocks = n_lat // blk
    assert q_dim % kv_dim == 0
    k_col = q_dim // kv_dim
    v_col = k_col + 1
    ctx_row = n_lat // n_ctx
    prv = lambda i: jnp.clip(i - 1, 0, n_lat_blocks - 1)
    own = lambda i: jnp.minimum(i, n_lat_blocks - 1)
    nxt = lambda i: jnp.minimum(i + 1, n_lat_blocks - 1)
    return pl.pallas_call(
        functools.partial(_attn_kernel, n_lat_blocks=n_lat_blocks, n_kv=n_kv),
        out_shape=jax.ShapeDtypeStruct((m, q_dim), BF16),
        grid=(n_blocks,),
        in_specs=[
            pl.BlockSpec(memory_space=pltpu.SMEM),
            pl.BlockSpec((blk, q_dim), lambda i: (i, 0)),
            pl.BlockSpec((n_ctx, kv_dim), lambda i: (ctx_row, k_col)),
            pl.BlockSpec((n_ctx, kv_dim), lambda i: (ctx_row, v_col)),
            pl.BlockSpec((blk, kv_dim), lambda i: (prv(i), k_col)),
            pl.BlockSpec((blk, kv_dim), lambda i: (own(i), k_col)),
            pl.BlockSpec((blk, kv_dim), lambda i: (nxt(i), k_col)),
            pl.BlockSpec((blk, kv_dim), lambda i: (prv(i), v_col)),
            pl.BlockSpec((blk, kv_dim), lambda i: (own(i), v_col)),
            pl.BlockSpec((blk, kv_dim), lambda i: (nxt(i), v_col)),
        ],
        out_specs=pl.BlockSpec((blk, q_dim), lambda i: (i, 0)),
        compiler_params=_cparams(("parallel",)),
        name="window_attention",
    )(sinks, qkv, qkv, qkv, qkv, qkv, qkv, qkv, qkv, qkv)


def _rope_tables(n_lat, n_ctx):
    rows = n_lat // GRID_W
    half = ATTN_HEAD_DIM // 2
    row = jnp.repeat(jnp.arange(rows, dtype=F32), GRID_W)
    col = jnp.tile(jnp.arange(GRID_W, dtype=F32), rows)
    inv_freq = ROPE_THETA ** (-jnp.arange(0, half, 2, dtype=F32) / half)
    ar = row[:, None] * inv_freq[None, :]
    ac = col[:, None] * inv_freq[None, :]
    cos = jnp.concatenate([jnp.cos(ar), jnp.cos(ar), jnp.cos(ac), jnp.cos(ac)], axis=-1)
    sin = jnp.concatenate([-jnp.sin(ar), jnp.sin(ar), -jnp.sin(ac), jnp.sin(ac)], axis=-1)
    cos = jnp.concatenate([cos, jnp.ones((n_ctx, ATTN_HEAD_DIM), F32)], axis=0)
    sin = jnp.concatenate([sin, jnp.zeros((n_ctx, ATTN_HEAD_DIM), F32)], axis=0)
    return cos, sin


def _rows8(*rows):
    d = rows[0].shape[-1]
    pad = [jnp.zeros((d,), F32)] * (8 - len(rows))
    return jnp.stack(list(rows) + pad, axis=0)


def kernel(x, c, ctx, c_ctx, ada_w, ada_b, norm1_w, norm2_w, ssd_w_in, ssd_conv_w, ssd_conv_b,
           ssd_dt_bias_f, ssd_dt_bias_b, ssd_a_log_f, ssd_a_log_b, ssd_d, ssd_norm_w, ssd_w_out,
           attn_w_qkv, attn_q_gain, attn_k_gain, attn_sinks, attn_w_o,
           ffn_w_up, ffn_conv_w, ffn_conv_b, ffn_w_down):
    assert x.shape[0] == 1 and ctx.shape[0] == 1
    n_lat, d = x.shape[1], x.shape[2]
    n_ctx = ctx.shape[1]
    m = n_lat + n_ctx
    depth = ada_w.shape[0]
    assert m % ROW_TILE == 0 and m % CONV_TILE == 0
    assert n_lat % n_ctx == 0 and n_lat % SSD_CHUNK == 0
    assert n_ctx % SSD_CHUNK == 0 and n_lat % GRID_W == 0

    xs = jnp.concatenate([x[0], ctx[0]], axis=0)
    c2 = jnp.stack([c[0], c_ctx], axis=1)
    mods = _modulation(c2, ada_w, ada_b).reshape(depth, 2, 6, d)

    di = ssd_norm_w.shape[1]
    n_heads = attn_sinks.shape[1]
    q_dim = n_heads * ATTN_HEAD_DIM
    cos, sin = _rope_tables(n_lat, n_ctx)

    w_in, w_out = ssd_w_in.astype(BF16), ssd_w_out.astype(BF16)
    w_qkv, w_o = attn_w_qkv.astype(BF16), attn_w_o.astype(BF16)
    w_up, w_down = ffn_w_up.astype(BF16), ffn_w_down.astype(BF16)
    ssd_cb = ssd_conv_b[:, None, :]
    ffn_cb = ffn_conv_b[:, None, :]

    for i in range(depth):
        last = i == depth - 1
        j = i // 2
        sh1, sc1, g1, sh2, sc2, g2 = [mods[i, :, q, :] for q in range(6)]
        tab1 = _rows8(norm1_w[i], sc1[0], sh1[0], sc1[1], sh1[1])
        tab2 = _rows8(norm2_w[i], sc2[0], sh2[0], sc2[1], sh2[1])
        gate1 = _rows8(g1[0], g1[1])
        gate2 = _rows8(g2[0], g2[1])
        if i % 2 == 0:
            zx, dt = _ssd_in_proj(xs, tab1, w_in, ssd_conv_w, ssd_cb, j, n_lat=n_lat, d_inner=di)
            bias_fb = jnp.concatenate([ssd_dt_bias_f[j], ssd_dt_bias_b[j]]).reshape(1, -1)
            alog_fb = jnp.concatenate([ssd_a_log_f[j], ssd_a_log_b[j]]).reshape(1, -1)
            dskip = jnp.repeat(ssd_d[j], SSD_HEAD_DIM).reshape(1, di)
            gated = _ssd_mixer(zx, dt, bias_fb, alog_fb, dskip, ssd_norm_w[j], n_lat=n_lat)
            xs = _matmul_residual(gated, w_out, xs, gate1, j, n_lat=n_lat, tn=512)
        else:
            gains = _rows8(attn_q_gain[j], attn_k_gain[j])
            qkv = _qkv_proj(xs, tab1, w_qkv, gains, cos, sin, j, n_lat=n_lat,
                            q_dim=q_dim, q_scale=ATTN_HEAD_DIM ** -0.5 * LOG2E)
            o = _attention(qkv, attn_sinks[j], n_lat=n_lat, n_ctx=n_ctx, n_heads=n_heads)
            xs = _matmul_residual(o, w_o, xs, gate1, j, n_lat=n_lat, tn=512)
        a = _ffn_up(xs, tab2, w_up, ffn_conv_w, ffn_cb, i, n_lat=n_lat)
        xs = _matmul_residual(a, w_down, xs, gate2, i, n_lat=n_lat, tn=512,
                              m_out=(n_lat if last else None))
    return xs[None]
```

```python
import functools

import jax
import jax.numpy as jnp
from jax import lax
from jax.experimental import pallas as pl
from jax.experimental.pallas import tpu as pltpu

F32 = jnp.float32
BF16 = jnp.bfloat16

NORM_EPS = 1e-6
GRID_W = 64
ROPE_THETA = 10000.0
LOG2E = 1.4426950408889634

SSD_HEAD_DIM = 64
SSD_GROUPS = 8
SSD_STATE = 128
SSD_CHUNK = 128
ATTN_HEAD_DIM = 128
ATTN_KV_HEADS = 4
ATTN_BLOCK = 128
ATTN_Q_PER_DOT = 4

ROW_TILE = 768
CONV_TILE = 1408
HALO = 16
QKV_ROWS = 352
DOT_CHUNKS = (288,) * 5
EPI_SLABS = (128,) * 11
VMEM_LIMIT = 56 * 1024 * 1024


def _cparams(sem):
    return pltpu.CompilerParams(dimension_semantics=sem, vmem_limit_bytes=VMEM_LIMIT)


def _silu(v):
    return v / (1.0 + jnp.exp(-v))


def _softplus(v):
    return jnp.maximum(v, 0.0) + jnp.log(1.0 + jnp.exp(-jnp.abs(v)))


def _modulation_kernel(c_ref, w_ref, b_ref, o_ref, *, kc):
    d = c_ref.shape[0]
    tn = w_ref.shape[2]

    def body(k, acc):
        a0, a1 = acc
        k0 = pl.multiple_of(k * kc, kc)
        sv = _silu(c_ref[pl.ds(k0, kc), :])
        w = w_ref[0, pl.ds(k0, kc), :]
        a0 = a0 + jnp.sum(w * sv[:, 0:1], axis=0, keepdims=True)
        a1 = a1 + jnp.sum(w * sv[:, 1:2], axis=0, keepdims=True)
        return a0, a1

    z = jnp.zeros((1, tn), F32)
    a0, a1 = lax.fori_loop(0, d // kc, body, (z, z))
    bias = b_ref[0]
    o_ref[0, 0:1, :] = a0 + bias
    o_ref[0, 1:2, :] = a1 + bias


def _modulation(c2, ada_w, ada_b):
    depth, d, n = ada_w.shape
    tn = 1024
    return pl.pallas_call(
        functools.partial(_modulation_kernel, kc=256),
        out_shape=jax.ShapeDtypeStruct((depth, 2, n), F32),
        grid=(depth, n // tn),
        in_specs=[
            pl.BlockSpec((d, 2), lambda l, j: (0, 0)),
            pl.BlockSpec((1, d, tn), lambda l, j: (l, 0, j)),
            pl.BlockSpec((1, 1, tn), lambda l, j: (l, 0, j)),
        ],
        out_specs=pl.BlockSpec((1, 2, tn), lambda l, j: (l, 0, j)),
        compiler_params=_cparams(("parallel", "parallel")),
        name="adaln_modulation",
    )(c2, ada_w, ada_b.reshape(depth, 1, n))


def _mod_vectors(tab_ref):
    w = tab_ref[0:1, :]
    return (w * (1.0 + tab_ref[1:2, :]), tab_ref[2:3, :],
            w * (1.0 + tab_ref[3:4, :]), tab_ref[4:5, :])


def _norm_mod_store(x, vecs, is_ctx, a_ref, dst0):
    mult = jnp.where(is_ctx, vecs[2], vecs[0])
    shift = jnp.where(is_ctx, vecs[3], vecs[1])
    ms = jnp.mean(x * x, axis=-1, keepdims=True)
    a_ref[pl.ds(dst0, x.shape[0]), :] = (x * lax.rsqrt(ms + NORM_EPS) * mult + shift).astype(BF16)


def _fill_rows(x_ref, tab_ref, a_ref, i, *, n_lat, dst_off, halos=None):
    tm = x_ref.shape[0]
    sub = 128
    vecs = _mod_vectors(tab_ref)

    def body(r, carry):
        r0 = pl.multiple_of(r * sub, sub)
        _norm_mod_store(x_ref[pl.ds(r0, sub), :], vecs, (i * tm + r0) >= n_lat, a_ref,
                        pl.multiple_of(dst_off + r0, HALO))
        return carry

    lax.fori_loop(0, tm // sub, body, 0)
    if halos is not None:
        xp_ref, xn_ref, m_total = halos
        _norm_mod_store(xp_ref[...], vecs, (i * tm - HALO) >= n_lat, a_ref, 0)
        _norm_mod_store(xn_ref[...], vecs, ((i + 1) * tm) >= n_lat, a_ref, dst_off + tm)
        start = i * tm

        @pl.when((start == 0) | (start == n_lat))
        def _():
            a_ref[0:HALO, :] = jnp.zeros((HALO, a_ref.shape[1]), BF16)

        @pl.when((start + tm == n_lat) | (start + tm == m_total))
        def _():
            a_ref[dst_off + tm:dst_off + tm + HALO, :] = jnp.zeros((HALO, a_ref.shape[1]), BF16)


def _conv_rows(stage_ref, cw_ref, cb_ref, r0, nrows, masks=None):
    ext = stage_ref[pl.ds(r0 - 8, nrows + 16), :]
    up = pltpu.roll(ext, 1, 0)[8:8 + nrows]
    v = ext[8:8 + nrows]
    dn = pltpu.roll(ext, nrows + 15, 0)[8:8 + nrows]
    if masks is not None:
        up = jnp.where(masks[0], up, 0.0)
        dn = jnp.where(masks[1], dn, 0.0)
    return cw_ref[0:1, :] * up + cw_ref[1:2, :] * v + cw_ref[2:3, :] * dn + cb_ref[...]


def _interleave(tm, dot_rows, epi_rows, slab_fix):
    assert sum(DOT_CHUNKS) == tm + 2 * HALO and sum(EPI_SLABS) == tm
    slabs = []
    r = 0
    for n in EPI_SLABS:
        slabs.append((r, n))
        r += n
    done = 0
    r0 = 0
    for nr in DOT_CHUNKS:
        dot_rows(r0, nr)
        while done < len(slabs) and HALO + sum(slabs[done]) + 1 <= r0:
            epi_rows(*slabs[done], None)
            done += 1
        r0 += nr
    for slab in slabs[done:]:
        epi_rows(*slab, None)
    slab_fix()


def _boundary_slab_fix(i, tm, n_lat, epi_rows):
    rb = n_lat % tm
    if rb:
        @pl.when(i == n_lat // tm)
        def _():
            rid = lax.broadcasted_iota(jnp.int32, (2 * HALO, 1), 0) + (rb - HALO)
            epi_rows(rb - HALO, 2 * HALO, (rid != rb, rid != rb - 1))


def _ssd_in_kernel(x_ref, xp_ref, xn_ref, tab_ref, w_ref, wdt_ref, cw_ref, cb_ref,
                   o_ref, dt_ref, a_ref, stage_ref, *, n_lat, m_total, n_z_tiles):
    i, j = pl.program_id(0), pl.program_id(1)
    tm = x_ref.shape[0]

    @pl.when(j == 0)
    def _():
        _fill_rows(x_ref, tab_ref, a_ref, i, n_lat=n_lat, dst_off=HALO,
                   halos=(xp_ref, xn_ref, m_total))
        dt_ref[...] = jnp.dot(a_ref[HALO:HALO + tm, :], wdt_ref[...], preferred_element_type=F32)

    def dot_rows(r0, nr):
        stage_ref[r0:r0 + nr, :] = jnp.dot(a_ref[r0:r0 + nr, :], w_ref[...],
                                           preferred_element_type=F32)

    @pl.when(j < n_z_tiles)
    def _():
        def epi(r, nr, masks):
            o_ref[r:r + nr, :] = _silu(stage_ref[HALO + r:HALO + r + nr, :]).astype(o_ref.dtype)

        _interleave(tm, dot_rows, epi, lambda: None)

    @pl.when(j >= n_z_tiles)
    def _():
        def epi(r, nr, masks):
            cv = _conv_rows(stage_ref, cw_ref, cb_ref, HALO + r, nr, masks)
            o_ref[r:r + nr, :] = _silu(cv).astype(o_ref.dtype)

        _interleave(tm, dot_rows, epi, lambda: _boundary_slab_fix(i, tm, n_lat, epi))


def _ssd_in_proj(x, tab, w_in, conv_w, conv_b, layer, *, n_lat, d_inner):
    m, d = x.shape
    tm, tn = CONV_TILE, 1024
    n_main = 2 * d_inner + 2 * SSD_GROUPS * SSD_STATE
    ndt = w_in.shape[2] - n_main
    n_z_tiles = d_inner // tn
    hb = tm // HALO
    last_hb = m // HALO - 1
    cmap = lambda i, j: (layer, 0, jnp.maximum(j - n_z_tiles, 0))
    return pl.pallas_call(
        functools.partial(_ssd_in_kernel, n_lat=n_lat, m_total=m, n_z_tiles=n_z_tiles),
        out_shape=(jax.ShapeDtypeStruct((m, n_main), BF16), jax.ShapeDtypeStruct((m, ndt), F32)),
        grid=(m // tm, n_main // tn),
        in_specs=[
            pl.BlockSpec((tm, d), lambda i, j: (i, 0)),
            pl.BlockSpec((HALO, d), lambda i, j: (jnp.maximum(i * hb - 1, 0), 0)),
            pl.BlockSpec((HALO, d), lambda i, j: (jnp.minimum((i + 1) * hb, last_hb), 0)),
            pl.BlockSpec((8, d), lambda i, j: (0, 0)),
            pl.BlockSpec((None, d, tn), lambda i, j: (layer, 0, j)),
            pl.BlockSpec((None, d, ndt), lambda i, j: (layer, 0, n_main // ndt)),
            pl.BlockSpec((None, 3, tn), cmap),
            pl.BlockSpec((None, 1, tn), cmap),
        ],
        out_specs=(pl.BlockSpec((tm, tn), lambda i, j: (i, j)),
                   pl.BlockSpec((tm, ndt), lambda i, j: (i, 0))),
        scratch_shapes=[pltpu.VMEM((tm + 2 * HALO, d), BF16),
                        pltpu.VMEM((tm + 2 * HALO, tn), F32)],
        compiler_params=_cparams(("parallel", "arbitrary")),
        name="ssd_in_proj",
    )(x, x, x, tab, w_in, w_in, conv_w, conv_b)


def _ffn_up_kernel(x_ref, xp_ref, xn_ref, tab_ref, wv_ref, wg_ref, cwv_ref, cwg_ref, cbv_ref,
                   cbg_ref, o_ref, a_ref, sv_ref, sg_ref, *, n_lat, m_total):
    i, j = pl.program_id(0), pl.program_id(1)
    tm = x_ref.shape[0]

    @pl.when(j == 0)
    def _():
        _fill_rows(x_ref, tab_ref, a_ref, i, n_lat=n_lat, dst_off=HALO,
                   halos=(xp_ref, xn_ref, m_total))

    def dot_rows(r0, nr):
        a = a_ref[r0:r0 + nr, :]
        sv_ref[r0:r0 + nr, :] = jnp.dot(a, wv_ref[...], preferred_element_type=F32)
        sg_ref[r0:r0 + nr, :] = jnp.dot(a, wg_ref[...], preferred_element_type=F32)

    def epi(r, nr, masks):
        val = _conv_rows(sv_ref, cwv_ref, cbv_ref, HALO + r, nr, masks)
        gat = _conv_rows(sg_ref, cwg_ref, cbg_ref, HALO + r, nr, masks)
        o_ref[r:r + nr, :] = (_silu(gat) * val).astype(o_ref.dtype)

    _interleave(tm, dot_rows, epi, lambda: _boundary_slab_fix(i, tm, n_lat, epi))


def _ffn_up(x, tab, w_up, conv_w, conv_b, layer, *, n_lat):
    m, d = x.shape
    f = w_up.shape[2] // 2
    tm, tn = CONV_TILE, 512
    nj = f // tn
    hb = tm // HALO
    last_hb = m // HALO - 1
    vmap = lambda i, j: (layer, 0, j)
    gmap = lambda i, j: (layer, 0, j + nj)
    return pl.pallas_call(
        functools.partial(_ffn_up_kernel, n_lat=n_lat, m_total=m),
        out_shape=jax.ShapeDtypeStruct((m, f), BF16),
        grid=(m // tm, nj),
        in_specs=[
            pl.BlockSpec((tm, d), lambda i, j: (i, 0)),
            pl.BlockSpec((HALO, d), lambda i, j: (jnp.maximum(i * hb - 1, 0), 0)),
            pl.BlockSpec((HALO, d), lambda i, j: (jnp.minimum((i + 1) * hb, last_hb), 0)),
            pl.BlockSpec((8, d), lambda i, j: (0, 0)),
            pl.BlockSpec((None, d, tn), vmap),
            pl.BlockSpec((None, d, tn), gmap),
            pl.BlockSpec((None, 3, tn), vmap),
            pl.BlockSpec((None, 3, tn), gmap),
            pl.BlockSpec((None, 1, tn), vmap),
            pl.BlockSpec((None, 1, tn), gmap),
        ],
        out_specs=pl.BlockSpec((tm, tn), lambda i, j: (i, j)),
        scratch_shapes=[pltpu.VMEM((tm + 2 * HALO, d), BF16),
                        pltpu.VMEM((tm + 2 * HALO, tn), F32),
                        pltpu.VMEM((tm + 2 * HALO, tn), F32)],
        compiler_params=_cparams(("parallel", "arbitrary")),
        name="convffn_up",
    )(x, x, x, tab, w_up, w_up, conv_w, conv_w, conv_b, conv_b)


def _rope_swap(t):
    lane = lax.broadcasted_iota(jnp.int32, t.shape, 1)
    first = (lane % 64) < 32
    return jnp.where(first, pltpu.roll(t, 96, 1), pltpu.roll(t, 32, 1))


def _qkv_kernel(x_ref, tab_ref, w_ref, gain_ref, cos_ref, sin_ref, o_ref, a_ref, *,
                n_lat, n_q_tiles, q_scale):
    i, j = pl.program_id(0), pl.program_id(1)
    hd = ATTN_HEAD_DIM

    @pl.when(j == 0)
    def _():
        _fill_rows(x_ref, tab_ref, a_ref, i, n_lat=n_lat, dst_off=0)

    tm = a_ref.shape[0]
    tn = w_ref.shape[1]
    chunk = QKV_ROWS
    assert tm % chunk == 0

    def product(r0):
        return jnp.dot(a_ref[r0:r0 + chunk, :], w_ref[...], preferred_element_type=F32)

    @pl.when(j <= n_q_tiles)
    def _():
        is_q = j < n_q_tiles
        gain = jnp.where(is_q, gain_ref[0:1, :], gain_ref[1:2, :])
        post = jnp.where(is_q, q_scale, 1.0)

        def finish(r0, acc):
            cos = cos_ref[r0:r0 + chunk, :]
            sin = sin_ref[r0:r0 + chunk, :]
            for h in range(tn // hd):
                t = acc[:, h * hd:(h + 1) * hd]
                ms = jnp.mean(t * t, axis=-1, keepdims=True)
                t = t * lax.rsqrt(ms + NORM_EPS) * gain
                t = (t * cos + _rope_swap(t) * sin) * post
                o_ref[r0:r0 + chunk, h * hd:(h + 1) * hd] = t.astype(o_ref.dtype)

        acc = product(0)
        for r0 in range(chunk, tm, chunk):
            nxt = product(r0)
            finish(r0 - chunk, acc)
            acc = nxt
        finish(tm - chunk, acc)

    @pl.when(j > n_q_tiles)
    def _():
        o_ref[...] = jnp.dot(a_ref[...], w_ref[...], preferred_element_type=F32).astype(o_ref.dtype)


def _qkv_proj(x, tab, w, gains, cos, sin, layer, *, n_lat, q_dim, q_scale):
    m, d = x.shape
    n = w.shape[2]
    tm, tn = CONV_TILE, 512
    assert (n - q_dim) == 2 * tn
    return pl.pallas_call(
        functools.partial(_qkv_kernel, n_lat=n_lat, n_q_tiles=q_dim // tn, q_scale=q_scale),
        out_shape=jax.ShapeDtypeStruct((m, n), BF16),
        grid=(m // tm, n // tn),
        in_specs=[pl.BlockSpec((tm, d), lambda i, j: (i, 0)),
                  pl.BlockSpec((8, d), lambda i, j: (0, 0)),
                  pl.BlockSpec((None, d, tn), lambda i, j: (layer, 0, j)),
                  pl.BlockSpec((8, ATTN_HEAD_DIM), lambda i, j: (0, 0)),
                  pl.BlockSpec((tm, ATTN_HEAD_DIM), lambda i, j: (i, 0)),
                  pl.BlockSpec((tm, ATTN_HEAD_DIM), lambda i, j: (i, 0))],
        out_specs=pl.BlockSpec((tm, tn), lambda i, j: (i, j)),
        scratch_shapes=[pltpu.VMEM((tm, d), BF16)],
        compiler_params=_cparams(("parallel", "arbitrary")),
        name="attn_qkv_proj",
    )(x, tab, w, gains, cos, sin)


def _mm_res_kernel(a_ref, w_ref, x_ref, g_ref, o_ref, *, n_lat):
    i = pl.program_id(0)
    tm = a_ref.shape[0]
    acc = jnp.dot(a_ref[...], w_ref[...], preferred_element_type=F32)
    rows = i * tm + lax.broadcasted_iota(jnp.int32, (tm, 1), 0)
    gate = jnp.where(rows >= n_lat, g_ref[1:2, :], g_ref[0:1, :])
    o_ref[...] = x_ref[...] + gate * acc


def _matmul_residual(a, w, x, gates, layer, *, n_lat, tn, m_out=None):
    m, k = a.shape
    n = w.shape[2]
    tm = ROW_TILE
    return pl.pallas_call(
        functools.partial(_mm_res_kernel, n_lat=n_lat),
        out_shape=jax.ShapeDtypeStruct((m if m_out is None else m_out, n), F32),
        grid=(m // tm, n // tn),
        in_specs=[pl.BlockSpec((tm, k), lambda i, j: (i, 0)),
                  pl.BlockSpec((None, k, tn), lambda i, j: (layer, 0, j)),
                  pl.BlockSpec((tm, tn), lambda i, j: (i, j)),
                  pl.BlockSpec((8, tn), lambda i, j: (0, j))],
        out_specs=pl.BlockSpec((tm, tn), lambda i, j: (i, j)),
        compiler_params=_cparams(("parallel", "parallel")),
        name="matmul_gated_residual",
    )(a, w, x, gates)


def _ssd_scan_chunk(xs_ref, bc_ref, dt_raw, bias_ref, alog_ref, y_ref, state_ref, col_ref, rowt_ref,
                    *, reverse, hoff, first):
    L = SSD_CHUNK
    P = SSD_HEAD_DIM
    G = SSD_GROUPS
    N = SSD_STATE
    H = xs_ref.shape[1] // P
    E = H // G
    EP = E * P
    PAIR = 2 * P
    assert PAIR == 128 and E % 2 == 0

    @pl.when(first)
    def _():
        state_ref[...] = jnp.zeros_like(state_ref)

    dt = _softplus(dt_raw + bias_ref[...])
    da2 = (-jnp.exp(alog_ref[...]) * dt) * LOG2E
    li = lax.broadcasted_iota(jnp.int32, (L, L), 0)
    si = lax.broadcasted_iota(jnp.int32, (L, L), 1)
    keep = (si >= li) if reverse else (si <= li)
    acs2 = jnp.dot(keep.astype(F32), da2, preferred_element_type=F32,
                   precision=lax.Precision.HIGHEST)
    edge = 0 if reverse else L - 1
    a2t = acs2.T
    r2t = a2t - jnp.log2(dt.T)
    edge_b = jnp.broadcast_to(a2t[:, edge:edge + 1], (128, L))
    rowt_ref[0] = r2t
    rowt_ref[1] = jnp.exp2(edge_b - r2t)
    rowt_ref[2] = jnp.exp2(edge_b)
    for g in range(G):
        lo = hoff + g * E
        col_ref[g, :, 0:E] = acs2[:, lo:lo + E]

    lane_lo = lax.broadcasted_iota(jnp.int32, (L, PAIR), 1) < P
    lane_lo1 = lax.broadcasted_iota(jnp.int32, (1, PAIR), 1) < P
    nt = (((1,), (1,)), ((), ()))

    def group(g, carry):
        b_off = pl.multiple_of(g * N, N)
        c_off = pl.multiple_of((G + g) * N, N)
        bg = bc_ref[:, pl.ds(b_off, N)]
        cg = bc_ref[:, pl.ds(c_off, N)]
        cb = lax.dot_general(cg, bg, nt, preferred_element_type=F32)
        bgt = bg.astype(F32).T
        h_in = state_ref[g]
        y_off = jnp.dot(cg, h_in.astype(BF16), preferred_element_type=F32)
        cols = col_ref[g]
        hrow = pl.multiple_of(hoff + g * E, E)
        rows_r = rowt_ref[0, pl.ds(hrow, E), :]
        rows_f = rowt_ref[1, pl.ds(hrow, E), :]
        rows_c = rowt_ref[2, pl.ds(hrow, E), :]
        x_off = pl.multiple_of(g * EP, EP)
        for pr in range(E // 2):
            e1, e2 = 2 * pr, 2 * pr + 1
            colb1 = jnp.broadcast_to(cols[:, e1:e1 + 1], (L, L))
            colb2 = jnp.broadcast_to(cols[:, e2:e2 + 1], (L, L))
            w1 = (cb * jnp.exp2(jnp.where(keep, colb1 - rows_r[e1:e1 + 1, :], -jnp.inf))).astype(BF16)
            w2 = (cb * jnp.exp2(jnp.where(keep, colb2 - rows_r[e2:e2 + 1, :], -jnp.inf))).astype(BF16)
            s1 = (bgt * rows_f[e1:e1 + 1, :]).astype(BF16)
            s2 = (bgt * rows_f[e2:e2 + 1, :]).astype(BF16)
            lhs = jnp.concatenate([jnp.concatenate([w1, w2], axis=1),
                                   jnp.concatenate([s1, s2], axis=1)], axis=0)
            xp = xs_ref[:, pl.ds(x_off + pr * PAIR, PAIR)]
            zero = jnp.zeros_like(xp)
            xbd = jnp.concatenate([jnp.where(lane_lo, xp, zero), jnp.where(lane_lo, zero, xp)],
                                  axis=0)
            res = jnp.dot(lhs, xbd, preferred_element_type=F32)
            e_in = jnp.exp2(jnp.where(lane_lo, colb1, colb2))
            y_ref[:, pl.ds(x_off + pr * PAIR, PAIR)] = (
                res[0:L] + y_off[:, pr * PAIR:(pr + 1) * PAIR] * e_in)
            cd = jnp.where(lane_lo1, rows_c[e1:e1 + 1, :], rows_c[e2:e2 + 1, :])
            state_ref[g, :, pr * PAIR:(pr + 1) * PAIR] = (
                cd * h_in[:, pr * PAIR:(pr + 1) * PAIR] + res[L:2 * L])
        return carry

    lax.fori_loop(0, G, group, 0, unroll=4)


def _ssd_fwd_kernel(xs_ref, bc_ref, dt_ref, bias_ref, alog_ref, y_ref,
                    state_ref, col_ref, rowt_ref, yacc_ref):
    t = pl.program_id(0)
    _ssd_scan_chunk(xs_ref, bc_ref, dt_ref[...], bias_ref, alog_ref, yacc_ref, state_ref, col_ref,
                    rowt_ref, reverse=False, hoff=0, first=(t == 0))
    y_ref[...] = yacc_ref[...].astype(y_ref.dtype)


def _ssd_bwd_kernel(xs_ref, bc_ref, dt_ref, yf_ref, sz_ref, dskip_ref, nw_ref, bias_ref, alog_ref,
                    g_ref, state_ref, col_ref, rowt_ref, yacc_ref, *, hoff):
    t = pl.program_id(0)
    _ssd_scan_chunk(xs_ref, bc_ref, dt_ref[...], bias_ref, alog_ref, yacc_ref, state_ref, col_ref,
                    rowt_ref, reverse=True, hoff=hoff, first=(t == 0))
    y = (yacc_ref[...] + yf_ref[...].astype(F32) + xs_ref[...].astype(F32) * dskip_ref[...])
    y = y * sz_ref[...].astype(F32)
    ms = jnp.mean(y * y, axis=-1, keepdims=True)
    g_ref[...] = (y * lax.rsqrt(ms + NORM_EPS) * nw_ref[...]).astype(g_ref.dtype)


def _ssd_mixer(zx, dt, bias_fb, alog_fb, dskip, norm_w, *, n_lat):
    m = zx.shape[0]
    di = norm_w.shape[0]
    gn2 = 2 * SSD_GROUPS * SSD_STATE
    L = SSD_CHUNK
    n_chunks = m // L
    n_lat_chunks = n_lat // L
    x_cb = 1
    bc_cb = (2 * di) // gn2
    cf = lambda t: (t + n_lat_chunks) % n_chunks
    scratch = [pltpu.VMEM((SSD_GROUPS, SSD_STATE, di // SSD_GROUPS), F32),
               pltpu.VMEM((SSD_GROUPS, L, 128), F32),
               pltpu.VMEM((3, 128, L), F32),
               pltpu.VMEM((L, di), F32)]
    vec = lambda width: pl.BlockSpec((1, width), lambda t: (0, 0))
    yf = pl.pallas_call(
        _ssd_fwd_kernel,
        out_shape=jax.ShapeDtypeStruct((m, di), BF16),
        grid=(n_chunks,),
        in_specs=[
            pl.BlockSpec((L, di), lambda t: (cf(t), x_cb)),
            pl.BlockSpec((L, gn2), lambda t: (cf(t), bc_cb)),
            pl.BlockSpec((L, 128), lambda t: (cf(t), 0)),
            vec(128), vec(128),
        ],
        out_specs=pl.BlockSpec((L, di), lambda t: (cf(t), 0)),
        scratch_shapes=scratch,
        compiler_params=_cparams(("arbitrary",)),
        name="ssd_scan_fwd",
    )(zx, zx, dt, bias_fb, alog_fb)

    cbk = lambda t: n_chunks - 1 - t
    return pl.pallas_call(
        functools.partial(_ssd_bwd_kernel, hoff=di // SSD_HEAD_DIM),
        out_shape=jax.ShapeDtypeStruct((m, di), BF16),
        grid=(n_chunks,),
        in_specs=[
            pl.BlockSpec((L, di), lambda t: (cbk(t), x_cb)),
            pl.BlockSpec((L, gn2), lambda t: (cbk(t), bc_cb)),
            pl.BlockSpec((L, 128), lambda t: (cbk(t), 0)),
            pl.BlockSpec((L, di), lambda t: (cbk(t), 0)),
            pl.BlockSpec((L, di), lambda t: (cbk(t), 0)),
            vec(di), vec(di), vec(128), vec(128),
        ],
        out_specs=pl.BlockSpec((L, di), lambda t: (cbk(t), 0)),
        scratch_shapes=scratch,
        compiler_params=_cparams(("arbitrary",)),
        name="ssd_scan_bwd_finish",
    )(zx, zx, dt, yf, zx, dskip, norm_w.reshape(1, di), bias_fb, alog_fb)


def _attn_kernel(sink_ref, q_ref, kc_ref, vc_ref, kp_ref, ko_ref, kn_ref, vp_ref, vo_ref, vn_ref,
                 o_ref, *, n_lat_blocks, n_kv):
    qb = pl.program_id(0)
    blk = ATTN_BLOCK
    hd = ATTN_HEAD_DIM
    grp = q_ref.shape[1] // (hd * n_kv)
    sub = ATTN_Q_PER_DOT
    assert grp % sub == 0
    rows = sub * blk
    n_ctx = kc_ref.shape[0]
    qi = lax.broadcasted_iota(jnp.int32, (rows, blk), 0) % blk
    kj = lax.broadcasted_iota(jnp.int32, (rows, blk), 1)
    is_lat = qb < n_lat_blocks
    keep_p = (is_lat & (qb > 0)) & (kj >= qi)
    keep_n = (is_lat & (qb < n_lat_blocks - 1)) & (kj <= qi)
    gid = lax.broadcasted_iota(jnp.int32, (rows, 1), 0) // blk
    nt = (((1,), (1,)), ((), ()))
    neg = -jnp.inf
    for kv in range(n_kv):
        ks = slice(kv * hd, (kv + 1) * hd)
        k_all = jnp.concatenate([kc_ref[:, ks], kp_ref[:, ks], ko_ref[:, ks], kn_ref[:, ks]], axis=0)
        v_all = jnp.concatenate([vc_ref[:, ks], vp_ref[:, ks], vo_ref[:, ks], vn_ref[:, ks]], axis=0)
        for h0 in range(kv * grp, (kv + 1) * grp, sub):
            q = jnp.concatenate([q_ref[:, (h0 + g) * hd:(h0 + g + 1) * hd] for g in range(sub)],
                                axis=0)
            s = lax.dot_general(q, k_all, nt, preferred_element_type=F32)
            s_c = [s[:, c0:c0 + blk] for c0 in range(0, n_ctx, blk)]
            s_p = jnp.where(keep_p, s[:, n_ctx:n_ctx + blk], neg)
            s_o = jnp.where(is_lat, s[:, n_ctx + blk:n_ctx + 2 * blk], neg)
            s_n = jnp.where(keep_n, s[:, n_ctx + 2 * blk:n_ctx + 3 * blk], neg)
            tiles = s_c + [s_p, s_o, s_n]
            sink = jnp.zeros((rows, 1), F32)
            for g in range(sub):
                sink = jnp.where(gid == g, sink_ref[h0 + g] * LOG2E, sink)
            mx_el = tiles[0]
            for t in tiles[1:]:
                mx_el = jnp.maximum(mx_el, t)
            mx = jnp.maximum(jnp.max(mx_el, axis=-1, keepdims=True), sink)
            mx_b = jnp.broadcast_to(mx, (rows, blk))
            ps = [jnp.exp2(t - mx_b) for t in tiles]
            sum_el = ps[0]
            for t in ps[1:]:
                sum_el = sum_el + t
            den = jnp.exp2(sink - mx) + jnp.sum(sum_el, axis=-1, keepdims=True)
            p = jnp.concatenate([t.astype(BF16) for t in ps], axis=1)
            o = jnp.dot(p, v_all, preferred_element_type=F32) / den
            for g in range(sub):
                c0 = (h0 + g) * hd
                o_ref[:, c0:c0 + hd] = o[g * blk:(g + 1) * blk, :].astype(o_ref.dtype)


def _attention(qkv, sinks, *, n_lat, n_ctx, n_heads):
    m = qkv.shape[0]
    blk = ATTN_BLOCK
    hd = ATTN_HEAD_DIM
    n_kv = ATTN_KV_HEADS
    q_dim = n_heads * hd
    kv_dim = n_kv * hd
    n_blocks = m // blk
    n_lat_blocks = n_lat // blk
    assert q_dim % kv_dim == 0
    k_col = q_dim // kv_dim
    v_col = k_col + 1
    ctx_row = n_lat // n_ctx
    prv = lambda i: jnp.clip(i - 1, 0, n_lat_blocks - 1)
    own = lambda i: jnp.minimum(i, n_lat_blocks - 1)
    nxt = lambda i: jnp.minimum(i + 1, n_lat_blocks - 1)
    return pl.pallas_call(
        functools.partial(_attn_kernel, n_lat_blocks=n_lat_blocks, n_kv=n_kv),
        out_shape=jax.ShapeDtypeStruct((m, q_dim), BF16),
        grid=(n_blocks,),
        in_specs=[
            pl.BlockSpec(memory_space=pltpu.SMEM),
            pl.BlockSpec((blk, q_dim), lambda i: (i, 0)),
            pl.BlockSpec((n_ctx, kv_dim), lambda i: (ctx_row, k_col)),
            pl.BlockSpec((n_ctx, kv_dim), lambda i: (ctx_row, v_col)),
            pl.BlockSpec((blk, kv_dim), lambda i: (prv(i), k_col)),
            pl.BlockSpec((blk, kv_dim), lambda i: (own(i), k_col)),
            pl.BlockSpec((blk, kv_dim), lambda i: (nxt(i), k_col)),
            pl.BlockSpec((blk, kv_dim), lambda i: (prv(i), v_col)),
            pl.BlockSpec((blk, kv_dim), lambda i: (own(i), v_col)),
            pl.BlockSpec((blk, kv_dim), lambda i: (nxt(i), v_col)),
        ],
        out_specs=pl.BlockSpec((blk, q_dim), lambda i: (i, 0)),
        compiler_params=_cparams(("parallel",)),
        name="window_attention",
    )(sinks, qkv, qkv, qkv, qkv, qkv, qkv, qkv, qkv, qkv)


def _rope_tables(n_lat, n_ctx):
    rows = n_lat // GRID_W
    half = ATTN_HEAD_DIM // 2
    row = jnp.repeat(jnp.arange(rows, dtype=F32), GRID_W)
    col = jnp.tile(jnp.arange(GRID_W, dtype=F32), rows)
    inv_freq = ROPE_THETA ** (-jnp.arange(0, half, 2, dtype=F32) / half)
    ar = row[:, None] * inv_freq[None, :]
    ac = col[:, None] * inv_freq[None, :]
    cos = jnp.concatenate([jnp.cos(ar), jnp.cos(ar), jnp.cos(ac), jnp.cos(ac)], axis=-1)
    sin = jnp.concatenate([-jnp.sin(ar), jnp.sin(ar), -jnp.sin(ac), jnp.sin(ac)], axis=-1)
    cos = jnp.concatenate([cos, jnp.ones((n_ctx, ATTN_HEAD_DIM), F32)], axis=0)
    sin = jnp.concatenate([sin, jnp.zeros((n_ctx, ATTN_HEAD_DIM), F32)], axis=0)
    return cos, sin


def _rows8(*rows):
    d = rows[0].shape[-1]
    pad = [jnp.zeros((d,), F32)] * (8 - len(rows))
    return jnp.stack(list(rows) + pad, axis=0)


def kernel(x, c, ctx, c_ctx, ada_w, ada_b, norm1_w, norm2_w, ssd_w_in, ssd_conv_w, ssd_conv_b,
           ssd_dt_bias_f, ssd_dt_bias_b, ssd_a_log_f, ssd_a_log_b, ssd_d, ssd_norm_w, ssd_w_out,
           attn_w_qkv, attn_q_gain, attn_k_gain, attn_sinks, attn_w_o,
           ffn_w_up, ffn_conv_w, ffn_conv_b, ffn_w_down):
    assert x.shape[0] == 1 and ctx.shape[0] == 1
    n_lat, d = x.shape[1], x.shape[2]
    n_ctx = ctx.shape[1]
    m = n_lat + n_ctx
    depth = ada_w.shape[0]
    assert m % ROW_TILE == 0 and m % CONV_TILE == 0
    assert n_lat % n_ctx == 0 and n_lat % SSD_CHUNK == 0
    assert n_ctx % SSD_CHUNK == 0 and n_lat % GRID_W == 0

    xs = jnp.concatenate([x[0], ctx[0]], axis=0)
    c2 = jnp.stack([c[0], c_ctx], axis=1)
    mods = _modulation(c2, ada_w, ada_b).reshape(depth, 2, 6, d)

    di = ssd_norm_w.shape[1]
    n_heads = attn_sinks.shape[1]
    q_dim = n_heads * ATTN_HEAD_DIM
    cos, sin = _rope_tables(n_lat, n_ctx)

    w_in, w_out = ssd_w_in.astype(BF16), ssd_w_out.astype(BF16)
    w_qkv, w_o = attn_w_qkv.astype(BF16), attn_w_o.astype(BF16)
    w_up, w_down = ffn_w_up.astype(BF16), ffn_w_down.astype(BF16)
    ssd_cb = ssd_conv_b[:, None, :]
    ffn_cb = ffn_conv_b[:, None, :]

    for i in range(depth):
        last = i == depth - 1
        j = i // 2
        sh1, sc1, g1, sh2, sc2, g2 = [mods[i, :, q, :] for q in range(6)]
        tab1 = _rows8(norm1_w[i], sc1[0], sh1[0], sc1[1], sh1[1])
        tab2 = _rows8(norm2_w[i], sc2[0], sh2[0], sc2[1], sh2[1])
        gate1 = _rows8(g1[0], g1[1])
        gate2 = _rows8(g2[0], g2[1])
        if i % 2 == 0:
            zx, dt = _ssd_in_proj(xs, tab1, w_in, ssd_conv_w, ssd_cb, j, n_lat=n_lat, d_inner=di)
            bias_fb = jnp.concatenate([ssd_dt_bias_f[j], ssd_dt_bias_b[j]]).reshape(1, -1)
            alog_fb = jnp.concatenate([ssd_a_log_f[j], ssd_a_log_b[j]]).reshape(1, -1)
            dskip = jnp.repeat(ssd_d[j], SSD_HEAD_DIM).reshape(1, di)
            gated = _ssd_mixer(zx, dt, bias_fb, alog_fb, dskip, ssd_norm_w[j], n_lat=n_lat)
            xs = _matmul_residual(gated, w_out, xs, gate1, j, n_lat=n_lat, tn=512)
        else:
            gains = _rows8(attn_q_gain[j], attn_k_gain[j])
            qkv = _qkv_proj(xs, tab1, w_qkv, gains, cos, sin, j, n_lat=n_lat,
                            q_dim=q_dim, q_scale=ATTN_HEAD_DIM ** -0.5 * LOG2E)
            o = _attention(qkv, attn_sinks[j], n_lat=n_lat, n_ctx=n_ctx, n_heads=n_heads)
            xs = _matmul_residual(o, w_o, xs, gate1, j, n_lat=n_lat, tn=d)
        a = _ffn_up(xs, tab2, w_up, ffn_conv_w, ffn_cb, i, n_lat=n_lat)
        xs = _matmul_residual(a, w_down, xs, gate2, i, n_lat=n_lat, tn=512,
                              m_out=(n_lat if last else None))
    return xs[None]
```

```python
import functools

import jax
import jax.numpy as jnp
from jax import lax
from jax.experimental import pallas as pl
from jax.experimental.pallas import tpu as pltpu

F32 = jnp.float32
BF16 = jnp.bfloat16

NORM_EPS = 1e-6
GRID_W = 64
ROPE_THETA = 10000.0
LOG2E = 1.4426950408889634

SSD_HEAD_DIM = 64
SSD_GROUPS = 8
SSD_STATE = 128
SSD_CHUNK = 128
ATTN_HEAD_DIM = 128
ATTN_KV_HEADS = 4
ATTN_BLOCK = 128
ATTN_Q_PER_DOT = 4

ROW_TILE = 768
CONV_TILE = 1408
HALO = 16
QKV_ROWS = 352
DOT_CHUNKS = (288,) * 5
EPI_SLABS = (128,) * 11
VMEM_LIMIT = 56 * 1024 * 1024


def _cparams(sem):
    return pltpu.CompilerParams(dimension_semantics=sem, vmem_limit_bytes=VMEM_LIMIT)


def _silu(v):
    return v / (1.0 + jnp.exp(-v))


def _softplus(v):
    return jnp.maximum(v, 0.0) + jnp.log(1.0 + jnp.exp(-jnp.abs(v)))


def _modulation_kernel(c_ref, w_ref, b_ref, o_ref, *, kc):
    d = c_ref.shape[0]
    tn = w_ref.shape[2]

    def body(k, acc):
        a0, a1 = acc
        k0 = pl.multiple_of(k * kc, kc)
        sv = _silu(c_ref[pl.ds(k0, kc), :])
        w = w_ref[0, pl.ds(k0, kc), :]
        a0 = a0 + jnp.sum(w * sv[:, 0:1], axis=0, keepdims=True)
        a1 = a1 + jnp.sum(w * sv[:, 1:2], axis=0, keepdims=True)
        return a0, a1

    z = jnp.zeros((1, tn), F32)
    a0, a1 = lax.fori_loop(0, d // kc, body, (z, z))
    bias = b_ref[0]
    o_ref[0, 0:1, :] = a0 + bias
    o_ref[0, 1:2, :] = a1 + bias


def _modulation(c2, ada_w, ada_b):
    depth, d, n = ada_w.shape
    tn = 1024
    return pl.pallas_call(
        functools.partial(_modulation_kernel, kc=256),
        out_shape=jax.ShapeDtypeStruct((depth, 2, n), F32),
        grid=(depth, n // tn),
        in_specs=[
            pl.BlockSpec((d, 2), lambda l, j: (0, 0)),
            pl.BlockSpec((1, d, tn), lambda l, j: (l, 0, j)),
            pl.BlockSpec((1, 1, tn), lambda l, j: (l, 0, j)),
        ],
        out_specs=pl.BlockSpec((1, 2, tn), lambda l, j: (l, 0, j)),
        compiler_params=_cparams(("parallel", "parallel")),
        name="adaln_modulation",
    )(c2, ada_w, ada_b.reshape(depth, 1, n))


def _mod_vectors(tab_ref):
    w = tab_ref[0:1, :]
    return (w * (1.0 + tab_ref[1:2, :]), tab_ref[2:3, :],
            w * (1.0 + tab_ref[3:4, :]), tab_ref[4:5, :])


def _norm_mod_store(x, vecs, is_ctx, a_ref, dst0):
    mult = jnp.where(is_ctx, vecs[2], vecs[0])
    shift = jnp.where(is_ctx, vecs[3], vecs[1])
    ms = jnp.mean(x * x, axis=-1, keepdims=True)
    a_ref[pl.ds(dst0, x.shape[0]), :] = (x * lax.rsqrt(ms + NORM_EPS) * mult + shift).astype(BF16)


def _fill_rows(x_ref, tab_ref, a_ref, i, *, n_lat, dst_off, halos=None):
    tm = x_ref.shape[0]
    sub = 128
    vecs = _mod_vectors(tab_ref)

    def body(r, carry):
        r0 = pl.multiple_of(r * sub, sub)
        _norm_mod_store(x_ref[pl.ds(r0, sub), :], vecs, (i * tm + r0) >= n_lat, a_ref,
                        pl.multiple_of(dst_off + r0, HALO))
        return carry

    lax.fori_loop(0, tm // sub, body, 0)
    if halos is not None:
        xp_ref, xn_ref, m_total = halos
        _norm_mod_store(xp_ref[...], vecs, (i * tm - HALO) >= n_lat, a_ref, 0)
        _norm_mod_store(xn_ref[...], vecs, ((i + 1) * tm) >= n_lat, a_ref, dst_off + tm)
        start = i * tm

        @pl.when((start == 0) | (start == n_lat))
        def _():
            a_ref[0:HALO, :] = jnp.zeros((HALO, a_ref.shape[1]), BF16)

        @pl.when((start + tm == n_lat) | (start + tm == m_total))
        def _():
            a_ref[dst_off + tm:dst_off + tm + HALO, :] = jnp.zeros((HALO, a_ref.shape[1]), BF16)


def _conv_rows(stage_ref, cw_ref, cb_ref, r0, nrows, masks=None):
    ext = stage_ref[pl.ds(r0 - 8, nrows + 16), :]
    up = pltpu.roll(ext, 1, 0)[8:8 + nrows]
    v = ext[8:8 + nrows]
    dn = pltpu.roll(ext, nrows + 15, 0)[8:8 + nrows]
    if masks is not None:
        up = jnp.where(masks[0], up, 0.0)
        dn = jnp.where(masks[1], dn, 0.0)
    return cw_ref[0:1, :] * up + cw_ref[1:2, :] * v + cw_ref[2:3, :] * dn + cb_ref[...]


def _interleave(tm, dot_rows, epi_rows, slab_fix):
    assert sum(DOT_CHUNKS) == tm + 2 * HALO and sum(EPI_SLABS) == tm
    slabs = []
    r = 0
    for n in EPI_SLABS:
        slabs.append((r, n))
        r += n
    done = 0
    r0 = 0
    for nr in DOT_CHUNKS:
        dot_rows(r0, nr)
        while done < len(slabs) and HALO + sum(slabs[done]) + 1 <= r0:
            epi_rows(*slabs[done], None)
            done += 1
        r0 += nr
    for slab in slabs[done:]:
        epi_rows(*slab, None)
    slab_fix()


def _boundary_slab_fix(i, tm, n_lat, epi_rows):
    rb = n_lat % tm
    if rb:
        @pl.when(i == n_lat // tm)
        def _():
            rid = lax.broadcasted_iota(jnp.int32, (2 * HALO, 1), 0) + (rb - HALO)
            epi_rows(rb - HALO, 2 * HALO, (rid != rb, rid != rb - 1))


def _ssd_in_kernel(x_ref, xp_ref, xn_ref, tab_ref, w_ref, wdt_ref, cw_ref, cb_ref,
                   o_ref, dt_ref, a_ref, stage_ref, *, n_lat, m_total, n_z_tiles):
    i, j = pl.program_id(0), pl.program_id(1)
    tm = x_ref.shape[0]

    @pl.when(j == 0)
    def _():
        _fill_rows(x_ref, tab_ref, a_ref, i, n_lat=n_lat, dst_off=HALO,
                   halos=(xp_ref, xn_ref, m_total))
        dt_ref[...] = jnp.dot(a_ref[HALO:HALO + tm, :], wdt_ref[...], preferred_element_type=F32)

    def dot_rows(r0, nr):
        stage_ref[r0:r0 + nr, :] = jnp.dot(a_ref[r0:r0 + nr, :], w_ref[...],
                                           preferred_element_type=F32)

    @pl.when(j < n_z_tiles)
    def _():
        def epi(r, nr, masks):
            o_ref[r:r + nr, :] = _silu(stage_ref[HALO + r:HALO + r + nr, :]).astype(o_ref.dtype)

        _interleave(tm, dot_rows, epi, lambda: None)

    @pl.when(j >= n_z_tiles)
    def _():
        def epi(r, nr, masks):
            cv = _conv_rows(stage_ref, cw_ref, cb_ref, HALO + r, nr, masks)
            o_ref[r:r + nr, :] = _silu(cv).astype(o_ref.dtype)

        _interleave(tm, dot_rows, epi, lambda: _boundary_slab_fix(i, tm, n_lat, epi))


def _ssd_in_proj(x, tab, w_in, conv_w, conv_b, layer, *, n_lat, d_inner):
    m, d = x.shape
    tm, tn = CONV_TILE, 1024
    n_main = 2 * d_inner + 2 * SSD_GROUPS * SSD_STATE
    ndt = w_in.shape[2] - n_main
    n_z_tiles = d_inner // tn
    hb = tm // HALO
    last_hb = m // HALO - 1
    cmap = lambda i, j: (layer, 0, jnp.maximum(j - n_z_tiles, 0))
    return pl.pallas_call(
        functools.partial(_ssd_in_kernel, n_lat=n_lat, m_total=m, n_z_tiles=n_z_tiles),
        out_shape=(jax.ShapeDtypeStruct((m, n_main), BF16), jax.ShapeDtypeStruct((m, ndt), F32)),
        grid=(m // tm, n_main // tn),
        in_specs=[
            pl.BlockSpec((tm, d), lambda i, j: (i, 0)),
            pl.BlockSpec((HALO, d), lambda i, j: (jnp.maximum(i * hb - 1, 0), 0)),
            pl.BlockSpec((HALO, d), lambda i, j: (jnp.minimum((i + 1) * hb, last_hb), 0)),
            pl.BlockSpec((8, d), lambda i, j: (0, 0)),
            pl.BlockSpec((None, d, tn), lambda i, j: (layer, 0, j)),
            pl.BlockSpec((None, d, ndt), lambda i, j: (layer, 0, n_main // ndt)),
            pl.BlockSpec((None, 3, tn), cmap),
            pl.BlockSpec((None, 1, tn), cmap),
        ],
        out_specs=(pl.BlockSpec((tm, tn), lambda i, j: (i, j)),
                   pl.BlockSpec((tm, ndt), lambda i, j: (i, 0))),
        scratch_shapes=[pltpu.VMEM((tm + 2 * HALO, d), BF16),
                        pltpu.VMEM((tm + 2 * HALO, tn), F32)],
        compiler_params=_cparams(("parallel", "arbitrary")),
        name="ssd_in_proj",
    )(x, x, x, tab, w_in, w_in, conv_w, conv_b)


def _ffn_up_kernel(x_ref, xp_ref, xn_ref, tab_ref, wv_ref, wg_ref, cwv_ref, cwg_ref, cbv_ref,
                   cbg_ref, o_ref, a_ref, sv_ref, sg_ref, *, n_lat, m_total):
    i, j = pl.program_id(0), pl.program_id(1)
    tm = x_ref.shape[0]

    @pl.when(j == 0)
    def _():
        _fill_rows(x_ref, tab_ref, a_ref, i, n_lat=n_lat, dst_off=HALO,
                   halos=(xp_ref, xn_ref, m_total))

    def dot_rows(r0, nr):
        a = a_ref[r0:r0 + nr, :]
        sv_ref[r0:r0 + nr, :] = jnp.dot(a, wv_ref[...], preferred_element_type=F32)
        sg_ref[r0:r0 + nr, :] = jnp.dot(a, wg_ref[...], preferred_element_type=F32)

    def epi(r, nr, masks):
        val = _conv_rows(sv_ref, cwv_ref, cbv_ref, HALO + r, nr, masks)
        gat = _conv_rows(sg_ref, cwg_ref, cbg_ref, HALO + r, nr, masks)
        o_ref[r:r + nr, :] = (_silu(gat) * val).astype(o_ref.dtype)

    _interleave(tm, dot_rows, epi, lambda: _boundary_slab_fix(i, tm, n_lat, epi))


def _ffn_up(x, tab, w_up, conv_w, conv_b, layer, *, n_lat):
    m, d = x.shape
    f = w_up.shape[2] // 2
    tm, tn = CONV_TILE, 512
    nj = f // tn
    hb = tm // HALO
    last_hb = m // HALO - 1
    vmap = lambda i, j: (layer, 0, j)
    gmap = lambda i, j: (layer, 0, j + nj)
    return pl.pallas_call(
        functools.partial(_ffn_up_kernel, n_lat=n_lat, m_total=m),
        out_shape=jax.ShapeDtypeStruct((m, f), BF16),
        grid=(m // tm, nj),
        in_specs=[
            pl.BlockSpec((tm, d), lambda i, j: (i, 0)),
            pl.BlockSpec((HALO, d), lambda i, j: (jnp.maximum(i * hb - 1, 0), 0)),
            pl.BlockSpec((HALO, d), lambda i, j: (jnp.minimum((i + 1) * hb, last_hb), 0)),
            pl.BlockSpec((8, d), lambda i, j: (0, 0)),
            pl.BlockSpec((None, d, tn), vmap),
            pl.BlockSpec((None, d, tn), gmap),
            pl.BlockSpec((None, 3, tn), vmap),
            pl.BlockSpec((None, 3, tn), gmap),
            pl.BlockSpec((None, 1, tn), vmap),
            pl.BlockSpec((None, 1, tn), gmap),
        ],
        out_specs=pl.BlockSpec((tm, tn), lambda i, j: (i, j)),
        scratch_shapes=[pltpu.VMEM((tm + 2 * HALO, d), BF16),
                        pltpu.VMEM((tm + 2 * HALO, tn), F32),
                        pltpu.VMEM((tm + 2 * HALO, tn), F32)],
        compiler_params=_cparams(("parallel", "arbitrary")),
        name="convffn_up",
    )(x, x, x, tab, w_up, w_up, conv_w, conv_w, conv_b, conv_b)


def _rope_swap(t):
    lane = lax.broadcasted_iota(jnp.int32, t.shape, 1)
    first = (lane % 64) < 32
    return jnp.where(first, pltpu.roll(t, 96, 1), pltpu.roll(t, 32, 1))


def _qkv_kernel(x_ref, tab_ref, w_ref, gain_ref, cos_ref, sin_ref, o_ref, a_ref, *,
                n_lat, n_q_tiles, q_scale):
    i, j = pl.program_id(0), pl.program_id(1)
    hd = ATTN_HEAD_DIM

    @pl.when(j == 0)
    def _():
        _fill_rows(x_ref, tab_ref, a_ref, i, n_lat=n_lat, dst_off=0)

    tm = a_ref.shape[0]
    tn = w_ref.shape[1]
    chunk = QKV_ROWS
    assert tm % chunk == 0

    def product(r0):
        return jnp.dot(a_ref[r0:r0 + chunk, :], w_ref[...], preferred_element_type=F32)

    @pl.when(j <= n_q_tiles)
    def _():
        is_q = j < n_q_tiles
        gain = jnp.where(is_q, gain_ref[0:1, :], gain_ref[1:2, :])
        post = jnp.where(is_q, q_scale, 1.0)

        def finish(r0, acc):
            cos = cos_ref[r0:r0 + chunk, :]
            sin = sin_ref[r0:r0 + chunk, :]
            for h in range(tn // hd):
                t = acc[:, h * hd:(h + 1) * hd]
                ms = jnp.mean(t * t, axis=-1, keepdims=True)
                t = t * lax.rsqrt(ms + NORM_EPS) * gain
                t = (t * cos + _rope_swap(t) * sin) * post
                o_ref[r0:r0 + chunk, h * hd:(h + 1) * hd] = t.astype(o_ref.dtype)

        acc = product(0)
        for r0 in range(chunk, tm, chunk):
            nxt = product(r0)
            finish(r0 - chunk, acc)
            acc = nxt
        finish(tm - chunk, acc)

    @pl.when(j > n_q_tiles)
    def _():
        o_ref[...] = jnp.dot(a_ref[...], w_ref[...], preferred_element_type=F32).astype(o_ref.dtype)


def _qkv_proj(x, tab, w, gains, cos, sin, layer, *, n_lat, q_dim, q_scale):
    m, d = x.shape
    n = w.shape[2]
    tm, tn = CONV_TILE, 512
    assert (n - q_dim) == 2 * tn
    return pl.pallas_call(
        functools.partial(_qkv_kernel, n_lat=n_lat, n_q_tiles=q_dim // tn, q_scale=q_scale),
        out_shape=jax.ShapeDtypeStruct((m, n), BF16),
        grid=(m // tm, n // tn),
        in_specs=[pl.BlockSpec((tm, d), lambda i, j: (i, 0)),
                  pl.BlockSpec((8, d), lambda i, j: (0, 0)),
                  pl.BlockSpec((None, d, tn), lambda i, j: (layer, 0, j)),
                  pl.BlockSpec((8, ATTN_HEAD_DIM), lambda i, j: (0, 0)),
                  pl.BlockSpec((tm, ATTN_HEAD_DIM), lambda i, j: (i, 0)),
                  pl.BlockSpec((tm, ATTN_HEAD_DIM), lambda i, j: (i, 0))],
        out_specs=pl.BlockSpec((tm, tn), lambda i, j: (i, j)),
        scratch_shapes=[pltpu.VMEM((tm, d), BF16)],
        compiler_params=_cparams(("parallel", "arbitrary")),
        name="attn_qkv_proj",
    )(x, tab, w, gains, cos, sin)


def _mm_res_kernel(a_ref, w_ref, x_ref, g_ref, o_ref, *, n_lat):
    i = pl.program_id(0)
    tm = a_ref.shape[0]
    acc = jnp.dot(a_ref[...], w_ref[...], preferred_element_type=F32)
    rows = i * tm + lax.broadcasted_iota(jnp.int32, (tm, 1), 0)
    gate = jnp.where(rows >= n_lat, g_ref[1:2, :], g_ref[0:1, :])
    o_ref[...] = x_ref[...] + gate * acc


def _matmul_residual(a, w, x, gates, layer, *, n_lat, tn, m_out=None):
    m, k = a.shape
    n = w.shape[2]
    tm = ROW_TILE
    return pl.pallas_call(
        functools.partial(_mm_res_kernel, n_lat=n_lat),
        out_shape=jax.ShapeDtypeStruct((m if m_out is None else m_out, n), F32),
        grid=(m // tm, n // tn),
        in_specs=[pl.BlockSpec((tm, k), lambda i, j: (i, 0)),
                  pl.BlockSpec((None, k, tn), lambda i, j: (layer, 0, j)),
                  pl.BlockSpec((tm, tn), lambda i, j: (i, j)),
                  pl.BlockSpec((8, tn), lambda i, j: (0, j))],
        out_specs=pl.BlockSpec((tm, tn), lambda i, j: (i, j)),
        compiler_params=_cparams(("parallel", "parallel")),
        name="matmul_gated_residual",
    )(a, w, x, gates)


def _ssd_scan_chunk(xs_ref, bc_ref, dt_raw, bias_ref, alog_ref, y_ref, state_ref, col_ref, rowt_ref,
                    *, reverse, hoff, first):
    L = SSD_CHUNK
    P = SSD_HEAD_DIM
    G = SSD_GROUPS
    N = SSD_STATE
    H = xs_ref.shape[1] // P
    E = H // G
    EP = E * P
    PAIR = 2 * P
    assert PAIR == 128 and E % 2 == 0

    @pl.when(first)
    def _():
        state_ref[...] = jnp.zeros_like(state_ref)

    dt = _softplus(dt_raw + bias_ref[...])
    da2 = (-jnp.exp(alog_ref[...]) * dt) * LOG2E
    li = lax.broadcasted_iota(jnp.int32, (L, L), 0)
    si = lax.broadcasted_iota(jnp.int32, (L, L), 1)
    keep = (si >= li) if reverse else (si <= li)
    acs2 = jnp.dot(keep.astype(F32), da2, preferred_element_type=F32,
                   precision=lax.Precision.HIGHEST)
    edge = 0 if reverse else L - 1
    a2t = acs2.T
    r2t = a2t - jnp.log2(dt.T)
    edge_b = jnp.broadcast_to(a2t[:, edge:edge + 1], (128, L))
    rowt_ref[0] = r2t
    rowt_ref[1] = jnp.exp2(edge_b - r2t)
    rowt_ref[2] = jnp.exp2(edge_b)
    for g in range(G):
        lo = hoff + g * E
        col_ref[g, :, 0:E] = acs2[:, lo:lo + E]

    lane_lo = lax.broadcasted_iota(jnp.int32, (L, PAIR), 1) < P
    lane_lo1 = lax.broadcasted_iota(jnp.int32, (1, PAIR), 1) < P
    nt = (((1,), (1,)), ((), ()))

    def group(g, carry):
        b_off = pl.multiple_of(g * N, N)
        c_off = pl.multiple_of((G + g) * N, N)
        bg = bc_ref[:, pl.ds(b_off, N)]
        cg = bc_ref[:, pl.ds(c_off, N)]
        cb = lax.dot_general(cg, bg, nt, preferred_element_type=F32)
        bgt = bg.astype(F32).T
        h_in = state_ref[g]
        y_off = jnp.dot(cg, h_in.astype(BF16), preferred_element_type=F32)
        cols = col_ref[g]
        hrow = pl.multiple_of(hoff + g * E, E)
        rows_r = rowt_ref[0, pl.ds(hrow, E), :]
        rows_f = rowt_ref[1, pl.ds(hrow, E), :]
        rows_c = rowt_ref[2, pl.ds(hrow, E), :]
        x_off = pl.multiple_of(g * EP, EP)
        for pr in range(E // 2):
            e1, e2 = 2 * pr, 2 * pr + 1
            colb1 = jnp.broadcast_to(cols[:, e1:e1 + 1], (L, L))
            colb2 = jnp.broadcast_to(cols[:, e2:e2 + 1], (L, L))
            w1 = (cb * jnp.exp2(jnp.where(keep, colb1 - rows_r[e1:e1 + 1, :], -jnp.inf))).astype(BF16)
            w2 = (cb * jnp.exp2(jnp.where(keep, colb2 - rows_r[e2:e2 + 1, :], -jnp.inf))).astype(BF16)
            s1 = (bgt * rows_f[e1:e1 + 1, :]).astype(BF16)
            s2 = (bgt * rows_f[e2:e2 + 1, :]).astype(BF16)
            lhs = jnp.concatenate([jnp.concatenate([w1, w2], axis=1),
                                   jnp.concatenate([s1, s2], axis=1)], axis=0)
            xp = xs_ref[:, pl.ds(x_off + pr * PAIR, PAIR)]
            zero = jnp.zeros_like(xp)
            xbd = jnp.concatenate([jnp.where(lane_lo, xp, zero), jnp.where(lane_lo, zero, xp)],
                                  axis=0)
            res = jnp.dot(lhs, xbd, preferred_element_type=F32)
            e_in = jnp.exp2(jnp.where(lane_lo, colb1, colb2))
            y_ref[:, pl.ds(x_off + pr * PAIR, PAIR)] = (
                res[0:L] + y_off[:, pr * PAIR:(pr + 1) * PAIR] * e_in)
            cd = jnp.where(lane_lo1, rows_c[e1:e1 + 1, :], rows_c[e2:e2 + 1, :])
            state_ref[g, :, pr * PAIR:(pr + 1) * PAIR] = (
                cd * h_in[:, pr * PAIR:(pr + 1) * PAIR] + res[L:2 * L])
        return carry

    lax.fori_loop(0, G, group, 0, unroll=8)


def _ssd_fwd_kernel(xs_ref, bc_ref, dt_ref, bias_ref, alog_ref, y_ref,
                    state_ref, col_ref, rowt_ref, yacc_ref):
    t = pl.program_id(0)
    _ssd_scan_chunk(xs_ref, bc_ref, dt_ref[...], bias_ref, alog_ref, yacc_ref, state_ref, col_ref,
                    rowt_ref, reverse=False, hoff=0, first=(t == 0))
    y_ref[...] = yacc_ref[...].astype(y_ref.dtype)


def _ssd_bwd_kernel(xs_ref, bc_ref, dt_ref, yf_ref, sz_ref, dskip_ref, nw_ref, bias_ref, alog_ref,
                    g_ref, state_ref, col_ref, rowt_ref, yacc_ref, *, hoff):
    t = pl.program_id(0)
    _ssd_scan_chunk(xs_ref, bc_ref, dt_ref[...], bias_ref, alog_ref, yacc_ref, state_ref, col_ref,
                    rowt_ref, reverse=True, hoff=hoff, first=(t == 0))
    y = (yacc_ref[...] + yf_ref[...].astype(F32) + xs_ref[...].astype(F32) * dskip_ref[...])
    y = y * sz_ref[...].astype(F32)
    ms = jnp.mean(y * y, axis=-1, keepdims=True)
    g_ref[...] = (y * lax.rsqrt(ms + NORM_EPS) * nw_ref[...]).astype(g_ref.dtype)


def _ssd_mixer(zx, dt, bias_fb, alog_fb, dskip, norm_w, *, n_lat):
    m = zx.shape[0]
    di = norm_w.shape[0]
    gn2 = 2 * SSD_GROUPS * SSD_STATE
    L = SSD_CHUNK
    n_chunks = m // L
    n_lat_chunks = n_lat // L
    x_cb = 1
    bc_cb = (2 * di) // gn2
    cf = lambda t: (t + n_lat_chunks) % n_chunks
    scratch = [pltpu.VMEM((SSD_GROUPS, SSD_STATE, di // SSD_GROUPS), F32),
               pltpu.VMEM((SSD_GROUPS, L, 128), F32),
               pltpu.VMEM((3, 128, L), F32),
               pltpu.VMEM((L, di), F32)]
    vec = lambda width: pl.BlockSpec((1, width), lambda t: (0, 0))
    yf = pl.pallas_call(
        _ssd_fwd_kernel,
        out_shape=jax.ShapeDtypeStruct((m, di), BF16),
        grid=(n_chunks,),
        in_specs=[
            pl.BlockSpec((L, di), lambda t: (cf(t), x_cb)),
            pl.BlockSpec((L, gn2), lambda t: (cf(t), bc_cb)),
            pl.BlockSpec((L, 128), lambda t: (cf(t), 0)),
            vec(128), vec(128),
        ],
        out_specs=pl.BlockSpec((L, di), lambda t: (cf(t), 0)),
        scratch_shapes=scratch,
        compiler_params=_cparams(("arbitrary",)),
        name="ssd_scan_fwd",
    )(zx, zx, dt, bias_fb, alog_fb)

    cbk = lambda t: n_chunks - 1 - t
    return pl.pallas_call(
        functools.partial(_ssd_bwd_kernel, hoff=di // SSD_HEAD_DIM),
        out_shape=jax.ShapeDtypeStruct((m, di), BF16),
        grid=(n_chunks,),
        in_specs=[
            pl.BlockSpec((L, di), lambda t: (cbk(t), x_cb)),
            pl.BlockSpec((L, gn2), lambda t: (cbk(t), bc_cb)),
            pl.BlockSpec((L, 128), lambda t: (cbk(t), 0)),
            pl.BlockSpec((L, di), lambda t: (cbk(t), 0)),
            pl.BlockSpec((L, di), lambda t: (cbk(t), 0)),
            vec(di), vec(di), vec(128), vec(128),
        ],
        out_specs=pl.BlockSpec((L, di), lambda t: (cbk(t), 0)),
        scratch_shapes=scratch,
        compiler_params=_cparams(("arbitrary",)),
        name="ssd_scan_bwd_finish",
    )(zx, zx, dt, yf, zx, dskip, norm_w.reshape(1, di), bias_fb, alog_fb)


def _attn_kernel(sink_ref, q_ref, kc_ref, vc_ref, kp_ref, ko_ref, kn_ref, vp_ref, vo_ref, vn_ref,
                 o_ref, *, n_lat_blocks, n_kv):
    qb = pl.program_id(0)
    blk = ATTN_BLOCK
    hd = ATTN_HEAD_DIM
    grp = q_ref.shape[1] // (hd * n_kv)
    sub = ATTN_Q_PER_DOT
    assert grp % sub == 0
    rows = sub * blk
    n_ctx = kc_ref.shape[0]
    qi = lax.broadcasted_iota(jnp.int32, (rows, blk), 0) % blk
    kj = lax.broadcasted_iota(jnp.int32, (rows, blk), 1)
    is_lat = qb < n_lat_blocks
    keep_p = (is_lat & (qb > 0)) & (kj >= qi)
    keep_n = (is_lat & (qb < n_lat_blocks - 1)) & (kj <= qi)
    gid = lax.broadcasted_iota(jnp.int32, (rows, 1), 0) // blk
    nt = (((1,), (1,)), ((), ()))
    neg = -jnp.inf
    for kv in range(n_kv):
        ks = slice(kv * hd, (kv + 1) * hd)
        k_all = jnp.concatenate([kc_ref[:, ks], kp_ref[:, ks], ko_ref[:, ks], kn_ref[:, ks]], axis=0)
        v_all = jnp.concatenate([vc_ref[:, ks], vp_ref[:, ks], vo_ref[:, ks], vn_ref[:, ks]], axis=0)
        for h0 in range(kv * grp, (kv + 1) * grp, sub):
            q = jnp.concatenate([q_ref[:, (h0 + g) * hd:(h0 + g + 1) * hd] for g in range(sub)],
                                axis=0)
            s = lax.dot_general(q, k_all, nt, preferred_element_type=F32)
            s_c = [s[:, c0:c0 + blk] for c0 in range(0, n_ctx, blk)]
            s_p = jnp.where(keep_p, s[:, n_ctx:n_ctx + blk], neg)
            s_o = jnp.where(is_lat, s[:, n_ctx + blk:n_ctx + 2 * blk], neg)
            s_n = jnp.where(keep_n, s[:, n_ctx + 2 * blk:n_ctx + 3 * blk], neg)
            tiles = s_c + [s_p, s_o, s_n]
            sink = jnp.zeros((rows, 1), F32)
            for g in range(sub):
                sink = jnp.where(gid == g, sink_ref[h0 + g] * LOG2E, sink)
            mx_el = tiles[0]
            for t in tiles[1:]:
                mx_el = jnp.maximum(mx_el, t)
            mx = jnp.maximum(jnp.max(mx_el, axis=-1, keepdims=True), sink)
            mx_b = jnp.broadcast_to(mx, (rows, blk))
            ps = [jnp.exp2(t - mx_b) for t in tiles]
            sum_el = ps[0]
            for t in ps[1:]:
                sum_el = sum_el + t
            den = jnp.exp2(sink - mx) + jnp.sum(sum_el, axis=-1, keepdims=True)
            p = jnp.concatenate([t.astype(BF16) for t in ps], axis=1)
            o = jnp.dot(p, v_all, preferred_element_type=F32) / den
            for g in range(sub):
                c0 = (h0 + g) * hd
                o_ref[:, c0:c0 + hd] = o[g * blk:(g + 1) * blk, :].astype(o_ref.dtype)


def _attention(qkv, sinks, *, n_lat, n_ctx, n_heads):
    m = qkv.shape[0]
    blk = ATTN_BLOCK
    hd = ATTN_HEAD_DIM
    n_kv = ATTN_KV_HEADS
    q_dim = n_heads * hd
    kv_dim = n_kv * hd
    n_blocks = m // blk
    n_lat_blocks = n_lat // blk
    assert q_dim % kv_dim == 0
    k_col = q_dim // kv_dim
    v_col = k_col + 1
    ctx_row = n_lat // n_ctx
    prv = lambda i: jnp.clip(i - 1, 0, n_lat_blocks - 1)
    own = lambda i: jnp.minimum(i, n_lat_blocks - 1)
    nxt = lambda i: jnp.minimum(i + 1, n_lat_blocks - 1)
    return pl.pallas_call(
        functools.partial(_attn_kernel, n_lat_blocks=n_lat_blocks, n_kv=n_kv),
        out_shape=jax.ShapeDtypeStruct((m, q_dim), BF16),
        grid=(n_blocks,),
        in_specs=[
            pl.BlockSpec(memory_space=pltpu.SMEM),
            pl.BlockSpec((blk, q_dim), lambda i: (i, 0)),
            pl.BlockSpec((n_ctx, kv_dim), lambda i: (ctx_row, k_col)),
            pl.BlockSpec((n_ctx, kv_dim), lambda i: (ctx_row, v_col)),
            pl.BlockSpec((blk, kv_dim), lambda i: (prv(i), k_col)),
            pl.BlockSpec((blk, kv_dim), lambda i: (own(i), k_col)),
            pl.BlockSpec((blk, kv_dim), lambda i: (nxt(i), k_col)),
            pl.BlockSpec((blk, kv_dim), lambda i: (prv(i), v_col)),
            pl.BlockSpec((blk, kv_dim), lambda i: (own(i), v_col)),
            pl.BlockSpec((blk, kv_dim), lambda i: (nxt(i), v_col)),
        ],
        out_specs=pl.BlockSpec((blk, q_dim), lambda i: (i, 0)),
        compiler_params=_cparams(("parallel",)),
        name="window_attention",
    )(sinks, qkv, qkv, qkv, qkv, qkv, qkv, qkv, qkv, qkv)


def _rope_tables(n_lat, n_ctx):
    rows = n_lat // GRID_W
    half = ATTN_HEAD_DIM // 2
    row = jnp.repeat(jnp.arange(rows, dtype=F32), GRID_W)
    col = jnp.tile(jnp.arange(GRID_W, dtype=F32), rows)
    inv_freq = ROPE_THETA ** (-jnp.arange(0, half, 2, dtype=F32) / half)
    ar = row[:, None] * inv_freq[None, :]
    ac = col[:, None] * inv_freq[None, :]
    cos = jnp.concatenate([jnp.cos(ar), jnp.cos(ar), jnp.cos(ac), jnp.cos(ac)], axis=-1)
    sin = jnp.concatenate([-jnp.sin(ar), jnp.sin(ar), -jnp.sin(ac), jnp.sin(ac)], axis=-1)
    cos = jnp.concatenate([cos, jnp.ones((n_ctx, ATTN_HEAD_DIM), F32)], axis=0)
    sin = jnp.concatenate([sin, jnp.zeros((n_ctx, ATTN_HEAD_DIM), F32)], axis=0)
    return cos, sin


def _rows8(*rows):
    d = rows[0].shape[-1]
    pad = [jnp.zeros((d,), F32)] * (8 - len(rows))
    return jnp.stack(list(rows) + pad, axis=0)


def kernel(x, c, ctx, c_ctx, ada_w, ada_b, norm1_w, norm2_w, ssd_w_in, ssd_conv_w, ssd_conv_b,
           ssd_dt_bias_f, ssd_dt_bias_b, ssd_a_log_f, ssd_a_log_b, ssd_d, ssd_norm_w, ssd_w_out,
           attn_w_qkv, attn_q_gain, attn_k_gain, attn_sinks, attn_w_o,
           ffn_w_up, ffn_conv_w, ffn_conv_b, ffn_w_down):
    assert x.shape[0] == 1 and ctx.shape[0] == 1
    n_lat, d = x.shape[1], x.shape[2]
    n_ctx = ctx.shape[1]
    m = n_lat + n_ctx
    depth = ada_w.shape[0]
    assert m % ROW_TILE == 0 and m % CONV_TILE == 0
    assert n_lat % n_ctx == 0 and n_lat % SSD_CHUNK == 0
    assert n_ctx % SSD_CHUNK == 0 and n_lat % GRID_W == 0

    xs = jnp.concatenate([x[0], ctx[0]], axis=0)
    c2 = jnp.stack([c[0], c_ctx], axis=1)
    mods = _modulation(c2, ada_w, ada_b).reshape(depth, 2, 6, d)

    di = ssd_norm_w.shape[1]
    n_heads = attn_sinks.shape[1]
    q_dim = n_heads * ATTN_HEAD_DIM
    cos, sin = _rope_tables(n_lat, n_ctx)

    w_in, w_out = ssd_w_in.astype(BF16), ssd_w_out.astype(BF16)
    w_qkv, w_o = attn_w_qkv.astype(BF16), attn_w_o.astype(BF16)
    w_up, w_down = ffn_w_up.astype(BF16), ffn_w_down.astype(BF16)
    ssd_cb = ssd_conv_b[:, None, :]
    ffn_cb = ffn_conv_b[:, None, :]

    for i in range(depth):
        last = i == depth - 1
        j = i // 2
        sh1, sc1, g1, sh2, sc2, g2 = [mods[i, :, q, :] for q in range(6)]
        tab1 = _rows8(norm1_w[i], sc1[0], sh1[0], sc1[1], sh1[1])
        tab2 = _rows8(norm2_w[i], sc2[0], sh2[0], sc2[1], sh2[1])
        gate1 = _rows8(g1[0], g1[1])
        gate2 = _rows8(g2[0], g2[1])
        if i % 2 == 0:
            zx, dt = _ssd_in_proj(xs, tab1, w_in, ssd_conv_w, ssd_cb, j, n_lat=n_lat, d_inner=di)
            bias_fb = jnp.concatenate([ssd_dt_bias_f[j], ssd_dt_bias_b[j]]).reshape(1, -1)
            alog_fb = jnp.concatenate([ssd_a_log_f[j], ssd_a_log_b[j]]).reshape(1, -1)
            dskip = jnp.repeat(ssd_d[j], SSD_HEAD_DIM).reshape(1, di)
            gated = _ssd_mixer(zx, dt, bias_fb, alog_fb, dskip, ssd_norm_w[j], n_lat=n_lat)
            xs = _matmul_residual(gated, w_out, xs, gate1, j, n_lat=n_lat, tn=1024)
        else:
            gains = _rows8(attn_q_gain[j], attn_k_gain[j])
            qkv = _qkv_proj(xs, tab1, w_qkv, gains, cos, sin, j, n_lat=n_lat,
                            q_dim=q_dim, q_scale=ATTN_HEAD_DIM ** -0.5 * LOG2E)
            o = _attention(qkv, attn_sinks[j], n_lat=n_lat, n_ctx=n_ctx, n_heads=n_heads)
            xs = _matmul_residual(o, w_o, xs, gate1, j, n_lat=n_lat, tn=d)
        a = _ffn_up(xs, tab2, w_up, ffn_conv_w, ffn_cb, i, n_lat=n_lat)
        xs = _matmul_residual(a, w_down, xs, gate2, i, n_lat=n_lat, tn=512,
                              m_out=(n_lat if last else None))
    return xs[None]
```
